```python
import jax, jax.numpy as jnp
from jax import lax
import numpy as np

D_MODEL = 1024
BATCH = 4
SEQ = 4096
DEPTH = 2

N_A = DEPTH // 2
N_B = DEPTH - N_A
D_FF = 2816
NORM_EPS = 1e-6

RW_HEAD_DIM = 64
RW_HEADS = D_MODEL // RW_HEAD_DIM
LORA_DECAY = 64
LORA_ICLR = 64
LORA_GATE = 160
RW_GN_EPS = 64e-5

NSA_HEADS = 16
NSA_GROUPS = 4
HEADS_PER_GROUP = NSA_HEADS // NSA_GROUPS
HEAD_DIM = D_MODEL // NSA_HEADS
L_CMP = 32
D_CMP = 16
L_SEL = 64
N_SEL = 16
WINDOW = 512
CMP_HIDDEN = 256
Q_BLK = 32
ROPE_THETA = 10000.0
FORCE_SCORE = 1e6
NEG_INF = -1e30

kernel_name = 'rwkv7_nsa_yoco_macaron'


def rmsnorm(x, g):
    x32 = x.astype(jnp.float32)
    y = x32 * lax.rsqrt(jnp.mean(x32 * x32, axis=-1, keepdims=True) + NORM_EPS)
    return y.astype(x.dtype) * g


def swiglu(x, w_gate, w_up, w_down):
    return (jax.nn.silu(x @ w_gate) * (x @ w_up)) @ w_down


def rope(x, pos):
    half = x.shape[-1] // 2
    inv = ROPE_THETA ** (-jnp.arange(half, dtype=jnp.float32) / half)
    ang = pos.astype(jnp.float32)[:, None] * inv[None, :]
    cos = jnp.cos(ang)[None, :, None, :].astype(x.dtype)
    sin = jnp.sin(ang)[None, :, None, :].astype(x.dtype)
    x1, x2 = x[..., :half], x[..., half:]
    return jnp.concatenate([x1 * cos - x2 * sin, x2 * cos + x1 * sin], axis=-1)


def masked_softmax(logits, mask):
    p = jax.nn.softmax(jnp.where(mask, logits.astype(jnp.float32), NEG_INF), axis=-1)
    return jnp.where(mask, p, 0.0)


def rwkv7_time_mix(x, mu, w_rkv, w0, w_lora_a, w_lora_b, a0, a_lora_a, a_lora_b,
                   g_lora_a, g_lora_b, k_k, k_a, r_k, gn_w, gn_b, w_o):
    B, S, D = x.shape
    H, N = RW_HEADS, RW_HEAD_DIM
    xx = jnp.pad(x, ((0, 0), (1, 0), (0, 0)))[:, :-1] - x
    x_r, x_w, x_k, x_v, x_a, x_g = [x + xx * mu[i] for i in range(6)]
    rkv = jnp.einsum('cbsd,cde->cbse', jnp.stack([x_r, x_k, x_v]), w_rkv)
    r, k, v = rkv[0], rkv[1], rkv[2]
    w = -jax.nn.softplus(-(w0 + jnp.tanh(x_w @ w_lora_a) @ w_lora_b)) - 0.5
    decay = jnp.exp(-jnp.exp(w.astype(jnp.float32)))
    a = jax.nn.sigmoid(a0 + (x_a @ a_lora_a) @ a_lora_b)
    g = jax.nn.sigmoid(x_g @ g_lora_a) @ g_lora_b
    kk = (k * k_k).reshape(B, S, H, N).astype(jnp.float32)
    kk = kk / jnp.maximum(jnp.sqrt(jnp.sum(kk * kk, axis=-1, keepdims=True)), 1e-12)
    k = k * (1.0 + (a - 1.0) * k_a)
    heads = lambda t: t.reshape(B, S, H, N)
    r, k, v, a, decay = heads(r), heads(k), heads(v), heads(a), heads(decay)

    def step(state, inp):
        r_t, w_t, k_t, v_t, kk_t, a_t = inp
        sa = jnp.einsum('bhij,bhj->bhi', state, -kk_t)
        state = (state * w_t[:, :, None, :]
                 + sa[..., None] * (kk_t * a_t)[:, :, None, :]
                 + v_t[..., :, None] * k_t[..., None, :])
        return state, jnp.einsum('bhij,bhj->bhi', state, r_t)

    seq_first = lambda t: jnp.moveaxis(t.astype(jnp.float32), 1, 0)
    state0 = jnp.zeros((B, H, N, N), jnp.float32)
    _, y = lax.scan(step, state0, (seq_first(r), seq_first(decay), seq_first(k),
                                    seq_first(v), seq_first(kk), seq_first(a)))
    y = jnp.moveaxis(y, 0, 1)
    mean = jnp.mean(y, axis=-1, keepdims=True)
    var = jnp.mean((y - mean) ** 2, axis=-1, keepdims=True)
    y = ((y - mean) * lax.rsqrt(var + RW_GN_EPS)).reshape(B, S, D).astype(x.dtype) * gn_w + gn_b
    bonus = jnp.sum(r * k * r_k, axis=-1, keepdims=True) * v
    y = y + bonus.reshape(B, S, D)
    return (y * g) @ w_o


def nsa_shared_kv(x, kv_norm, kv_w, kv_k_norm, cmp_pos_k, cmp_k_w1, cmp_k_w2,
                  cmp_pos_v, cmp_v_w1, cmp_v_w2):
    B, S, _ = x.shape
    G, dh = NSA_GROUPS, HEAD_DIM
    h = rmsnorm(x, kv_norm)
    kv = (h @ kv_w).reshape(B, S, 6, G, dh)
    k_cmp_tok, v_cmp_tok, k_sel, v_sel, k_win, v_win = [kv[:, :, i] for i in range(6)]
    pos = jnp.arange(S)
    k_sel = rope(rmsnorm(k_sel, kv_k_norm[1]), pos)
    k_win = rope(rmsnorm(k_win, kv_k_norm[2]), pos)
    n_cmp = (S - L_CMP) // D_CMP + 1
    idx = jnp.arange(n_cmp)[:, None] * D_CMP + jnp.arange(L_CMP)[None, :]

    def compress(tok, pos_emb, w1, w2):
        blocks = tok[:, idx] + pos_emb[:, None, :]
        blocks = jnp.moveaxis(blocks, 3, 2).reshape(B, n_cmp, G, L_CMP * dh)
        return jax.nn.gelu(blocks @ w1) @ w2

    k_cmp = rmsnorm(compress(k_cmp_tok, cmp_pos_k, cmp_k_w1, cmp_k_w2), kv_k_norm[0])
    v_cmp = compress(v_cmp_tok, cmp_pos_v, cmp_v_w1, cmp_v_w2)
    return k_cmp, v_cmp, k_sel, v_sel, k_win, v_win


def nsa_attention(h, shared, w_q, q_norm, w_o):
    k_cmp, v_cmp, k_sel, v_sel, k_win, v_win = shared
    B, S, _ = h.shape
    H, G, HPG, dh = NSA_HEADS, NSA_GROUPS, HEADS_PER_GROUP, HEAD_DIM
    qg = h @ w_q
    q = rmsnorm(qg[..., :H * dh].reshape(B, S, H, dh), q_norm)
    q_rot = rope(q, jnp.arange(S))
    gates = jax.nn.sigmoid(qg[..., H * dh:].reshape(B, S, H, 3))
    n_blk = S // Q_BLK
    nsel = S // L_SEL
    n_top = min(N_SEL, nsel)
    n_cmp = k_cmp.shape[1]
    scale = dh ** -0.5
    lpad = L_CMP // D_CMP - 1
    stride = L_SEL // D_CMP
    span = stride + lpad
    lim = stride * (nsel - 1) + 1
    rpad = stride * (nsel - 1) + span - lpad - n_cmp
    kb_sel = jnp.transpose(k_sel.reshape(B, nsel, L_SEL, G, dh), (0, 3, 1, 2, 4))
    vb_sel = jnp.transpose(v_sel.reshape(B, nsel, L_SEL, G, dh), (0, 3, 1, 2, 4))
    k_win_pad = jnp.pad(k_win, ((0, 0), (WINDOW, 0), (0, 0), (0, 0)))
    v_win_pad = jnp.pad(v_win, ((0, 0), (WINDOW, 0), (0, 0), (0, 0)))
    cmp_end = jnp.arange(n_cmp) * D_CMP + L_CMP - 1
    blk = jnp.arange(nsel)
    bi = jnp.arange(B)[:, None, None, None]
    gi = jnp.arange(G)[None, None, :, None]

    def to_blocks(t):
        return jnp.moveaxis(t.reshape(B, n_blk, Q_BLK, G, HPG, t.shape[-1]), 1, 0)

    def attend(args):
        qc, qrc, gc, ci = args
        s0 = ci * Q_BLK
        t = s0 + jnp.arange(Q_BLK)
        lc = jnp.einsum('bqghd,bngd->bqghn', qc, k_cmp) * scale
        p_cmp = masked_softmax(lc, (cmp_end[None, :] <= t[:, None])[None, :, None, None, :])
        o_cmp = jnp.einsum('bqghn,bngd->bqghd', p_cmp.astype(v_cmp.dtype), v_cmp)
        imp = jnp.pad(jnp.sum(p_cmp, axis=3), ((0, 0), (0, 0), (0, 0), (lpad, rpad)))
        p_slc = imp[..., 0:lim:stride]
        for o in range(1, span):
            p_slc = p_slc + imp[..., o:o + lim:stride]
        cur = t // L_SEL
        forced = (blk[None, :] == 0) | (blk[None, :] == cur[:, None]) | (blk[None, :] == cur[:, None] - 1)
        future = blk[None, :] * L_SEL > t[:, None]
        score = jnp.where(forced[None, :, None, :], FORCE_SCORE, p_slc)
        score = jnp.where(future[None, :, None, :], -1.0, score)
        _, sel = lax.top_k(score, n_top)
        kg = kb_sel[bi, gi, sel].reshape(B, Q_BLK, G, n_top * L_SEL, dh)
        vg = vb_sel[bi, gi, sel].reshape(B, Q_BLK, G, n_top * L_SEL, dh)
        tok = (sel[..., None] * L_SEL + jnp.arange(L_SEL)).reshape(B, Q_BLK, G, n_top * L_SEL)
        ls = jnp.einsum('bqghd,bqgkd->bqghk', qrc, kg) * scale
        p_sel = masked_softmax(ls, (tok <= t[None, :, None, None])[:, :, :, None, :])
        o_sel = jnp.einsum('bqghk,bqgkd->bqghd', p_sel.astype(vg.dtype), vg)
        kw = lax.dynamic_slice_in_dim(k_win_pad, s0, WINDOW + Q_BLK, axis=1)
        vw = lax.dynamic_slice_in_dim(v_win_pad, s0, WINDOW + Q_BLK, axis=1)
        kpos = s0 - WINDOW + jnp.arange(WINDOW + Q_BLK)
        mw = (kpos[None, :] <= t[:, None]) & (kpos[None, :] > t[:, None] - WINDOW) & (kpos[None, :] >= 0)
        lw = jnp.einsum('bqghd,bkgd->bqghk', qrc, kw) * scale
        p_win = masked_softmax(lw, mw[None, :, None, None, :])
        o_win = jnp.einsum('bqghk,bkgd->bqghd', p_win.astype(vw.dtype), vw)
        return gc[..., 0:1] * o_cmp + gc[..., 1:2] * o_sel + gc[..., 2:3] * o_win

    o = lax.map(attend, (to_blocks(q), to_blocks(q_rot), to_blocks(gates), jnp.arange(n_blk)))
    o = jnp.moveaxis(o, 0, 1).reshape(B, S, H * dh)
    return o @ w_o


def setup_inputs(seed: int = 0) -> dict:
    key = jax.random.key(seed)
    ks = iter(jax.random.split(key, 64))
    nrm = lambda shape, s: jax.random.normal(next(ks), shape, jnp.float32) * s
    uni = lambda shape, lo, hi: jax.random.uniform(next(ks), shape, jnp.float32, lo, hi)
    D, H, dh, G = D_MODEL, NSA_HEADS, HEAD_DIM, NSA_GROUPS
    return {
        'x': nrm((BATCH, SEQ, D), 1.0),
        'norm_ffn1': 1.0 + nrm((DEPTH, D), 0.02),
        'ffn1_w_gate': nrm((DEPTH, D, D_FF), D ** -0.5),
        'ffn1_w_up': nrm((DEPTH, D, D_FF), D ** -0.5),
        'ffn1_w_down': nrm((DEPTH, D_FF, D), D_FF ** -0.5),
        'norm_mix': 1.0 + nrm((DEPTH, D), 0.02),
        'norm_ffn2': 1.0 + nrm((DEPTH, D), 0.02),
        'ffn2_w_gate': nrm((DEPTH, D, D_FF), D ** -0.5),
        'ffn2_w_up': nrm((DEPTH, D, D_FF), D ** -0.5),
        'ffn2_w_down': nrm((DEPTH, D_FF, D), D_FF ** -0.5),
        'rw_mu': uni((N_A, 6, D), 0.0, 1.0),
        'rw_w_rkv': nrm((N_A, 3, D, D), D ** -0.5),
        'rw_w0': uni((N_A, D), -5.0, 1.0),
        'rw_w_lora_a': nrm((N_A, D, LORA_DECAY), D ** -0.5),
        'rw_w_lora_b': nrm((N_A, LORA_DECAY, D), 0.1 * LORA_DECAY ** -0.5),
        'rw_a0': nrm((N_A, D), 0.1),
        'rw_a_lora_a': nrm((N_A, D, LORA_ICLR), D ** -0.5),
        'rw_a_lora_b': nrm((N_A, LORA_ICLR, D), 0.5 * LORA_ICLR ** -0.5),
        'rw_g_lora_a': nrm((N_A, D, LORA_GATE), D ** -0.5),
        'rw_g_lora_b': nrm((N_A, LORA_GATE, D), LORA_GATE ** -0.5),
        'rw_k_k': 0.85 + nrm((N_A, D), 0.05),
        'rw_k_a': 1.0 + nrm((N_A, D), 0.05),
        'rw_r_k': nrm((N_A, RW_HEADS, RW_HEAD_DIM), 0.1),
        'rw_gn_w': 1.0 + nrm((N_A, D), 0.02),
        'rw_gn_b': nrm((N_A, D), 0.02),
        'rw_w_o': nrm((N_A, D, D), D ** -0.5),
        'kv_norm': 1.0 + nrm((D,), 0.02),
        'kv_w': nrm((D, 6 * G * dh), D ** -0.5),
        'kv_k_norm': 1.0 + nrm((3, dh), 0.02),
        'cmp_pos_k': nrm((L_CMP, dh), 0.5),
        'cmp_k_w1': nrm((L_CMP * dh, CMP_HIDDEN), (L_CMP * dh) ** -0.5),
        'cmp_k_w2': nrm((CMP_HIDDEN, dh), CMP_HIDDEN ** -0.5),
        'cmp_pos_v': nrm((L_CMP, dh), 0.5),
        'cmp_v_w1': nrm((L_CMP * dh, CMP_HIDDEN), (L_CMP * dh) ** -0.5),
        'cmp_v_w2': nrm((CMP_HIDDEN, dh), CMP_HIDDEN ** -0.5),
        'nsa_w_q': nrm((N_B, D, H * dh + 3 * H), D ** -0.5),
        'nsa_q_norm': 1.0 + nrm((N_B, dh), 0.02),
        'nsa_w_o': nrm((N_B, H * dh, D), (H * dh) ** -0.5),
    }


def reference(x, norm_ffn1, ffn1_w_gate, ffn1_w_up, ffn1_w_down, norm_mix, norm_ffn2,
              ffn2_w_gate, ffn2_w_up, ffn2_w_down, rw_mu, rw_w_rkv, rw_w0, rw_w_lora_a,
              rw_w_lora_b, rw_a0, rw_a_lora_a, rw_a_lora_b, rw_g_lora_a, rw_g_lora_b,
              rw_k_k, rw_k_a, rw_r_k, rw_gn_w, rw_gn_b, rw_w_o, kv_norm, kv_w, kv_k_norm,
              cmp_pos_k, cmp_k_w1, cmp_k_w2, cmp_pos_v, cmp_v_w1, cmp_v_w2,
              nsa_w_q, nsa_q_norm, nsa_w_o):
    shared = None
    for l in range(DEPTH):
        x = x + 0.5 * swiglu(rmsnorm(x, norm_ffn1[l]), ffn1_w_gate[l], ffn1_w_up[l], ffn1_w_down[l])
        h = rmsnorm(x, norm_mix[l])
        if l < N_A:
            i = l
            x = x + rwkv7_time_mix(h, rw_mu[i], rw_w_rkv[i], rw_w0[i], rw_w_lora_a[i], rw_w_lora_b[i],
                                   rw_a0[i], rw_a_lora_a[i], rw_a_lora_b[i], rw_g_lora_a[i],
                                   rw_g_lora_b[i], rw_k_k[i], rw_k_a[i], rw_r_k[i], rw_gn_w[i],
                                   rw_gn_b[i], rw_w_o[i])
        else:
            i = l - N_A
            x = x + nsa_attention(h, shared, nsa_w_q[i], nsa_q_norm[i], nsa_w_o[i])
        x = x + 0.5 * swiglu(rmsnorm(x, norm_ffn2[l]), ffn2_w_gate[l], ffn2_w_up[l], ffn2_w_down[l])
        if l == N_A - 1:
            shared = nsa_shared_kv(x, kv_norm, kv_w, kv_k_norm, cmp_pos_k, cmp_k_w1, cmp_k_w2,
                                   cmp_pos_v, cmp_v_w1, cmp_v_w2)
    return x
```

```python
import functools

import numpy as np
import jax
import jax.numpy as jnp
from jax import lax
from jax.experimental import pallas as pl
from jax.experimental.pallas import tpu as pltpu

F32 = jnp.float32
BF16 = jnp.bfloat16

NORM_EPS = 1e-6
RW_HEAD_DIM = 64
RW_GN_EPS = 64e-5
NSA_HEADS = 16
NSA_GROUPS = 4
HEADS_PER_GROUP = NSA_HEADS // NSA_GROUPS
HEAD_DIM = 64
L_CMP = 32
D_CMP = 16
L_SEL = 64
N_SEL = 16
WINDOW = 512
ROPE_THETA = 10000.0
FORCE_SCORE = 1e6
NEG_INF = -1e30

CHUNK = 64
INV_BLOCK = 16
Q_TILE = 128
VMEM_LIMIT = 56 * 1024 * 1024


def _cparams(sem):
    return pltpu.CompilerParams(dimension_semantics=sem, vmem_limit_bytes=VMEM_LIMIT)


def _dot(a, b):
    return jnp.dot(a, b, preferred_element_type=F32)


def _dot_nt(a, b):
    return lax.dot_general(a, b, (((1,), (1,)), ((), ())), preferred_element_type=F32)


def _dot_tn(a, b):
    return lax.dot_general(a, b, (((0,), (0,)), ((), ())), preferred_element_type=F32)


def _split2(x):
    hi = x.astype(BF16)
    lo = (x - hi.astype(F32)).astype(BF16)
    return hi, lo


def _split3(x):
    hi = x.astype(BF16)
    r1 = x - hi.astype(F32)
    mid = r1.astype(BF16)
    lo = (r1 - mid.astype(F32)).astype(BF16)
    return hi, mid, lo


def _dot3(a, b, dot=_dot):
    ah, al = _split2(a)
    bh, bl = _split2(b)
    return dot(ah, bh) + (dot(ah, bl) + dot(al, bh))


def _sel_right(x, m01):
    x1, x2, x3 = _split3(x)
    return _dot(x1, m01) + (_dot(x2, m01) + _dot(x3, m01))


def _sel_left(m01, x):
    x1, x2, x3 = _split3(x)
    return _dot(m01, x1) + (_dot(m01, x2) + _dot(m01, x3))


def _rms(x, g):
    return x * lax.rsqrt(jnp.mean(x * x, axis=-1, keepdims=True) + NORM_EPS) * g


def _sigmoid(z):
    return 1.0 / (1.0 + jnp.exp(-z))


def _rope(x, cos, sin_signed):
    n = x.shape[-1]
    half = HEAD_DIM // 2
    lane = lax.broadcasted_iota(jnp.int32, x.shape, 1)
    first = (lane & (HEAD_DIM - 1)) < half
    partner = jnp.where(first, pltpu.roll(x, n - half, axis=1), pltpu.roll(x, half, axis=1))
    return x * cos + partner * sin_signed


def _ffn_kernel(x_ref, g_ref, wg_ref, wu_ref, wd_ref, o_ref, h_ref, acc_ref):
    j = pl.program_id(1)

    @pl.when(j == 0)
    def _():
        h_ref[...] = _rms(x_ref[...], g_ref[...]).astype(BF16)
        acc_ref[...] = jnp.zeros_like(acc_ref)

    h = h_ref[...]
    gate = _dot(h, wg_ref[...])
    up = _dot(h, wu_ref[...])
    act = (gate * _sigmoid(gate) * up).astype(BF16)
    acc_ref[...] += _dot(act, wd_ref[...])

    @pl.when(j == pl.num_programs(1) - 1)
    def _():
        o_ref[...] = x_ref[...] + 0.5 * acc_ref[...]


def _ffn(x, g, wg, wu, wd, *, tm=1024, tf=256):
    T, D = x.shape
    F = wg.shape[1]
    tm = min(tm, T)
    return pl.pallas_call(
        _ffn_kernel,
        grid=(T // tm, F // tf),
        in_specs=[
            pl.BlockSpec((tm, D), lambda i, j: (i, 0)),
            pl.BlockSpec((1, D), lambda i, j: (0, 0)),
            pl.BlockSpec((D, tf), lambda i, j: (0, j)),
            pl.BlockSpec((D, tf), lambda i, j: (0, j)),
            pl.BlockSpec((tf, D), lambda i, j: (j, 0)),
        ],
        out_specs=pl.BlockSpec((tm, D), lambda i, j: (i, 0)),
        out_shape=jax.ShapeDtypeStruct((T, D), F32),
        scratch_shapes=[pltpu.VMEM((tm, D), BF16), pltpu.VMEM((tm, D), F32)],
        compiler_params=_cparams(("parallel", "arbitrary")),
        name="ffn",
    )(x, g.reshape(1, D), wg.astype(BF16), wu.astype(BF16), wd.astype(BF16))


def _rwkv_prep_kernel(x_ref, xp_ref, gm_ref, mu_ref, wrkv_ref, w0_ref, wla_ref, wlb_ref, a0_ref,
                      ala_ref, alb_ref, gla_ref, glb_ref,
                      r_ref, lw_ref, k_ref, v_ref, a_ref, g_ref, *, tiles_per_seq):
    i = pl.program_id(0)
    gm = gm_ref[...]
    h = _rms(x_ref[...], gm)
    hp = _rms(xp_ref[...], gm)[7:8, :]
    hp = jnp.where(i % tiles_per_seq == 0, 0.0, hp)
    row = lax.broadcasted_iota(jnp.int32, h.shape, 0)
    prev = jnp.where(row == 0, hp, pltpu.roll(h, 1, axis=0))
    xx = prev - h
    mu = mu_ref[...]
    mix = lambda c: (h + xx * mu[c:c + 1, :]).astype(BF16)
    r_ref[...] = _dot(mix(0), wrkv_ref[0])
    k_ref[...] = _dot(mix(2), wrkv_ref[1])
    v_ref[...] = _dot(mix(3), wrkv_ref[2])
    z = w0_ref[...] + _dot(jnp.tanh(_dot(mix(1), wla_ref[...])).astype(BF16), wlb_ref[...])
    softplus = jnp.maximum(-z, 0.0) + jnp.log(1.0 + jnp.exp(-jnp.abs(z)))
    lw_ref[...] = -jnp.exp(-softplus - 0.5)
    a_ref[...] = _sigmoid(a0_ref[...] + _dot(_dot(mix(4), ala_ref[...]).astype(BF16), alb_ref[...]))
    g_ref[...] = _dot(_sigmoid(_dot(mix(5), gla_ref[...])).astype(BF16), glb_ref[...])


def _rwkv_prep(x, gmix, mu, w_rkv, w0, wla, wlb, a0, ala, alb, gla, glb, *, seq, tm=256):
    T, D = x.shape
    tm = min(tm, seq)
    full = lambda a: pl.BlockSpec(a.shape, lambda i: (0,) * a.ndim)
    row = lambda a: a.reshape(1, D)
    args = [gmix.reshape(1, D), mu, w_rkv.astype(BF16), row(w0), wla.astype(BF16), wlb.astype(BF16),
            row(a0), ala.astype(BF16), alb.astype(BF16), gla.astype(BF16), glb.astype(BF16)]
    out = jax.ShapeDtypeStruct((T, D), F32)
    return pl.pallas_call(
        functools.partial(_rwkv_prep_kernel, tiles_per_seq=seq // tm),
        grid=(T // tm,),
        in_specs=[pl.BlockSpec((tm, D), lambda i: (i, 0)),
                  pl.BlockSpec((8, D), lambda i: (jnp.maximum(i * (tm // 8) - 1, 0), 0))]
                 + [full(a) for a in args],
        out_specs=[pl.BlockSpec((tm, D), lambda i: (i, 0))] * 6,
        out_shape=[out] * 6,
        compiler_params=_cparams(("parallel",)),
        name="rwkv_prep",
    )(x, x, *args)


def _rwkv_chunk(r, lw, k, v, a, kk_p, ka_p, rk_p, gn_w, gn_b, state, consts):
    tri_incl, tri_strict, blockdiag, eye, ltri = consts
    kkr = k * kk_p
    kk = kkr / jnp.maximum(jnp.sqrt(jnp.sum(kkr * kkr, axis=-1, keepdims=True)), 1e-12)
    k2 = k * (1.0 + (a - 1.0) * ka_p)
    kb = kk * a

    cum = _sel_left(ltri, lw)
    cum_last = cum[CHUNK - 1:CHUNK, :]
    einv = jnp.exp(-cum)
    elast = jnp.exp(cum_last - cum)
    a_t = -kk * jnp.exp(cum - lw)
    r_t = r * jnp.exp(cum)
    b_h = kb * einv
    k_h = k2 * einv
    b_l = kb * elast
    k_l = k2 * elast

    d3 = _dot3
    nt = functools.partial(_dot3, dot=_dot_nt)
    tn = functools.partial(_dot3, dot=_dot_tn)
    m_ab = jnp.where(tri_strict, nt(a_t, b_h), 0.0)
    m_ak = jnp.where(tri_strict, nt(a_t, k_h), 0.0)
    m_rb = jnp.where(tri_incl, nt(r_t, b_h), 0.0)
    m_rk = jnp.where(tri_incl, nt(r_t, k_h), 0.0)

    m_d = jnp.where(blockdiag, m_ab, 0.0)
    m_o = m_ab - m_d
    dinv = eye + m_d
    pw = m_d
    for _ in range(int(np.log2(INV_BLOCK)) - 1):
        pw = d3(pw, pw)
        dinv = dinv + d3(dinv, pw)
    n1 = d3(dinv, m_o)
    n2 = d3(n1, n1)
    rhs = jnp.concatenate([a_t, d3(m_ak, v)], axis=1)
    x = d3(dinv, rhs)
    x = x + d3(n2, x)
    x = x + d3(n1, x)

    z1 = d3(m_rb, x)
    z2 = tn(b_l, x)
    g1 = r_t + z1[:, :RW_HEAD_DIM]
    y_c = z1[:, RW_HEAD_DIM:] + d3(m_rk, v)
    trans = jnp.where(eye > 0, jnp.exp(cum_last), 0.0) + z2[:, :RW_HEAD_DIM]
    add = z2[:, RW_HEAD_DIM:] + tn(k_l, v)

    y = d3(g1, state) + y_c
    new_state = d3(trans, state) + add

    mean = jnp.mean(y, axis=-1, keepdims=True)
    dev = y - mean
    var = jnp.mean(dev * dev, axis=-1, keepdims=True)
    yn = dev * lax.rsqrt(var + RW_GN_EPS) * gn_w + gn_b
    bonus = jnp.sum(r * k2 * rk_p, axis=-1, keepdims=True) * v
    return yn + bonus, new_state


def _rwkv_scan_kernel(r_ref, lw_ref, k_ref, v_ref, a_ref, kk_ref, ka_ref, rk_ref, gw_ref, gb_ref, o_ref,
                      *, n_chunks, heads):
    N = RW_HEAD_DIM
    row = lax.broadcasted_iota(jnp.int32, (CHUNK, CHUNK), 0)
    col = lax.broadcasted_iota(jnp.int32, (CHUNK, CHUNK), 1)
    tri_incl = row >= col
    tri_strict = row > col
    blockdiag = (row // INV_BLOCK) == (col // INV_BLOCK)
    eye = jnp.where(row == col, 1.0, 0.0).astype(F32)
    ltri = jnp.where(tri_incl, 1.0, 0.0).astype(BF16)
    consts = (tri_incl, tri_strict, blockdiag, eye, ltri)

    def body(c, states):
        t0 = pl.multiple_of(c * CHUNK, CHUNK)
        outs, new_states = [], []
        for hh in range(heads):
            sl = slice(hh * N, (hh + 1) * N)
            ld = lambda ref: ref[pl.ds(t0, CHUNK), sl]
            y, st = _rwkv_chunk(ld(r_ref), ld(lw_ref), ld(k_ref), ld(v_ref), ld(a_ref),
                                kk_ref[:, sl], ka_ref[:, sl], rk_ref[:, sl], gw_ref[:, sl], gb_ref[:, sl],
                                states[hh], consts)
            outs.append(y)
            new_states.append(st)
        o_ref[pl.ds(t0, CHUNK), :] = jnp.concatenate(outs, axis=1)
        return tuple(new_states)

    lax.fori_loop(0, n_chunks, body, tuple(jnp.zeros((N, N), F32) for _ in range(heads)))


def _rwkv_scan(r, lw, k, v, a, k_k, k_a, r_k, gn_w, gn_b, *, batch, seq):
    T, D = r.shape
    heads = 2
    lanes = heads * RW_HEAD_DIM
    seq_spec = pl.BlockSpec((None, seq, lanes), lambda b, p: (b, 0, p))
    par_spec = pl.BlockSpec((1, lanes), lambda b, p: (0, p))
    as3 = lambda t: t.reshape(batch, seq, D)
    row = lambda t: t.reshape(1, D)
    y = pl.pallas_call(
        functools.partial(_rwkv_scan_kernel, n_chunks=seq // CHUNK, heads=heads),
        grid=(batch, D // lanes),
        in_specs=[seq_spec] * 5 + [par_spec] * 5,
        out_specs=seq_spec,
        out_shape=jax.ShapeDtypeStruct((batch, seq, D), F32),
        compiler_params=_cparams(("parallel", "parallel")),
        name="rwkv_scan",
    )(as3(r), as3(lw), as3(k), as3(v), as3(a), row(k_k), row(k_a), row(r_k), row(gn_w), row(gn_b))
    return y.reshape(T, D)


def _gated_out_kernel(x_ref, y_ref, g_ref, w_ref, o_ref):
    o_ref[...] = x_ref[...] + _dot((y_ref[...] * g_ref[...]).astype(BF16), w_ref[...])


def _proj_out_kernel(x_ref, y_ref, w_ref, o_ref):
    o_ref[...] = x_ref[...] + _dot(y_ref[...].astype(BF16), w_ref[...])


def _out_proj(x, y, w, g=None, *, tm=512):
    T, D = x.shape
    tm = min(tm, T)
    tile = pl.BlockSpec((tm, D), lambda i: (i, 0))
    wspec = pl.BlockSpec(w.shape, lambda i: (0, 0))
    if g is None:
        kern, ins, specs = _proj_out_kernel, (x, y, w.astype(BF16)), [tile, tile, wspec]
    else:
        kern, ins, specs = _gated_out_kernel, (x, y, g, w.astype(BF16)), [tile, tile, tile, wspec]
    return pl.pallas_call(
        kern, grid=(T // tm,), in_specs=specs, out_specs=tile,
        out_shape=jax.ShapeDtypeStruct((T, D), F32),
        compiler_params=_cparams(("parallel",)),
        name="out_proj",
    )(*ins)


def _group_rms(x, gain, pool, expand):
    ss = _sel_right(_sel_right(x * x, pool), expand)
    return x * lax.rsqrt(ss * (1.0 / HEAD_DIM) + NORM_EPS) * gain


def _nsa_kv_kernel(x_ref, gn_ref, w_ref, kn_ref, cos_ref, sin_ref, pool_ref, exp_ref,
                   kc_ref, vc_ref, ks_ref, vs_ref, kw_ref, vw_ref):
    h = _rms(x_ref[...], gn_ref[...]).astype(BF16)
    kv = _dot(h, w_ref[...])
    W = NSA_GROUPS * HEAD_DIM
    part = lambda i: kv[:, i * W:(i + 1) * W]
    cos, sin = cos_ref[...], sin_ref[...]
    pool, expand = pool_ref[...], exp_ref[...]
    kc_ref[...] = part(0)
    vc_ref[...] = part(1)
    ks_ref[...] = _rope(_group_rms(part(2), kn_ref[1:2, :], pool, expand), cos, sin).astype(BF16)
    vs_ref[...] = part(3).astype(BF16)
    kw_ref[...] = _rope(_group_rms(part(4), kn_ref[2:3, :], pool, expand), cos, sin).astype(BF16)
    vw_ref[...] = part(5).astype(BF16)


def _pool_matrices(width):
    head = np.arange(width) // HEAD_DIM
    pool = (head[:, None] == np.arange(128)[None, :]).astype(np.float32)
    return jnp.asarray(pool, BF16), jnp.asarray(pool.T, BF16)


def _rope_tables(seq, width):
    half = HEAD_DIM // 2
    inv = ROPE_THETA ** (-jnp.arange(half, dtype=F32) / half)
    ang = jnp.arange(seq).astype(F32)[:, None] * inv[None, :]
    cos, sin = jnp.cos(ang), jnp.sin(ang)
    reps = width // HEAD_DIM
    return (jnp.tile(jnp.concatenate([cos, cos], axis=1), (1, reps)),
            jnp.tile(jnp.concatenate([-sin, sin], axis=1), (1, reps)))


def _nsa_kv(x, kv_norm, kv_w, kv_k_norm, *, seq, tm=512):
    T, D = x.shape
    tm = min(tm, seq)
    W = NSA_GROUPS * HEAD_DIM
    cos, sin = _rope_tables(seq, W)
    pool, expand = _pool_matrices(W)
    kn = jnp.tile(kv_k_norm, (1, NSA_GROUPS))
    tps = seq // tm
    full = lambda a: pl.BlockSpec(a.shape, lambda i: (0,) * a.ndim)
    tab = pl.BlockSpec((tm, W), lambda i: (i % tps, 0))
    out_tile = pl.BlockSpec((tm, W), lambda i: (i, 0))
    w = kv_w.astype(BF16)
    return pl.pallas_call(
        _nsa_kv_kernel,
        grid=(T // tm,),
        in_specs=[pl.BlockSpec((tm, D), lambda i: (i, 0)), pl.BlockSpec((1, D), lambda i: (0, 0)),
                  full(w), full(kn), tab, tab, full(pool), full(expand)],
        out_specs=[out_tile] * 6,
        out_shape=[jax.ShapeDtypeStruct((T, W), F32)] * 2 + [jax.ShapeDtypeStruct((T, W), BF16)] * 4,
        compiler_params=_cparams(("parallel",)),
        name="nsa_kv",
    )(x, kv_norm.reshape(1, D), w, kn, cos, sin, pool, expand)


def _compress(c, pos, w1, w2):
    half = w1.shape[0] // 2
    lo = _dot((c + pos[0:1, :]).astype(BF16), w1[:half, :])
    hi = _dot((c + pos[1:2, :]).astype(BF16), w1[half:, :])
    hid = lo + pltpu.roll(hi, hi.shape[0] - 1, axis=0)
    act = 0.5 * hid * (1.0 + jnp.tanh(np.sqrt(2.0 / np.pi) * (hid + 0.044715 * hid * hid * hid)))
    return _dot(act.astype(BF16), w2)


def _nsa_cmp_kernel(ck_ref, cv_ref, pk_ref, pv_ref, w1k_ref, w2k_ref, w1v_ref, w2v_ref, kn_ref,
                    ko_ref, vo_ref):
    kc = _compress(ck_ref[...], pk_ref[...], w1k_ref[...], w2k_ref[...])
    ko_ref[...] = _rms(kc, kn_ref[...])
    vo_ref[...] = _compress(cv_ref[...], pv_ref[...], w1v_ref[...], w2v_ref[...])


def _nsa_cmp(kc_tok, vc_tok, kn0, pos_k, w1k, w2k, pos_v, w1v, w2v, *, batch, seq):
    G, dh = NSA_GROUPS, HEAD_DIM
    nch = seq // D_CMP
    cw = D_CMP * dh

    def chunks(t):
        t = t.reshape(batch, nch, D_CMP, G, dh)
        return jnp.transpose(t, (0, 3, 1, 2, 4)).reshape(batch * G, nch, cw)

    full = lambda a: pl.BlockSpec(a.shape, lambda i: (0,) * a.ndim)
    blk = pl.BlockSpec((None, nch, cw), lambda i: (i, 0, 0))
    oblk = pl.BlockSpec((None, nch, dh), lambda i: (i, 0, 0))
    args = [pos_k.reshape(2, cw), pos_v.reshape(2, cw), w1k.astype(BF16), w2k.astype(BF16),
            w1v.astype(BF16), w2v.astype(BF16), kn0.reshape(1, dh)]
    return pl.pallas_call(
        _nsa_cmp_kernel,
        grid=(batch * G,),
        in_specs=[blk, blk] + [full(a) for a in args],
        out_specs=[oblk, oblk],
        out_shape=[jax.ShapeDtypeStruct((batch * G, nch, dh), F32)] * 2,
        compiler_params=_cparams(("parallel",)),
        name="nsa_cmp",
    )(chunks(kc_tok), chunks(vc_tok), *args)


def _nsa_q_kernel(x_ref, gn_ref, wq_ref, wg_ref, qn_ref, cos_ref, sin_ref, pool_ref, exp_ref,
                  q_ref, qr_ref, gate_ref):
    h = _rms(x_ref[...], gn_ref[...]).astype(BF16)
    q = _group_rms(_dot(h, wq_ref[...]), qn_ref[...], pool_ref[...], exp_ref[...])
    reps = q.shape[1] // cos_ref.shape[1]
    cos = jnp.concatenate([cos_ref[...]] * reps, axis=1)
    sin = jnp.concatenate([sin_ref[...]] * reps, axis=1)
    scale = HEAD_DIM ** -0.5
    q_ref[...] = (q * scale).astype(BF16)
    qr_ref[...] = (_rope(q, cos, sin) * scale).astype(BF16)
    gate_ref[...] = _sigmoid(_dot(h, wg_ref[...]))


def _nsa_q(x, gmix, w_q, q_norm, *, seq, tm=512):
    T, D = x.shape
    tm = min(tm, seq)
    HD = NSA_HEADS * HEAD_DIM
    cos, sin = _rope_tables(seq, 128)
    pool, expand = _pool_matrices(HD)
    wq = w_q[:, :HD].astype(BF16)
    wg = jnp.pad(w_q[:, HD:], ((0, 0), (0, 128 - 3 * NSA_HEADS))).astype(BF16)
    qn = jnp.tile(q_norm.reshape(1, HEAD_DIM), (1, NSA_HEADS))
    tps = seq // tm
    full = lambda a: pl.BlockSpec(a.shape, lambda i: (0,) * a.ndim)
    tab = pl.BlockSpec((tm, 128), lambda i: (i % tps, 0))
    tile = lambda w: pl.BlockSpec((tm, w), lambda i: (i, 0))
    return pl.pallas_call(
        _nsa_q_kernel,
        grid=(T // tm,),
        in_specs=[tile(D), pl.BlockSpec((1, D), lambda i: (0, 0)), full(wq), full(wg), full(qn),
                  tab, tab, full(pool), full(expand)],
        out_specs=[tile(HD), tile(HD), tile(128)],
        out_shape=[jax.ShapeDtypeStruct((T, HD), BF16)] * 2 + [jax.ShapeDtypeStruct((T, 128), F32)],
        compiler_params=_cparams(("parallel",)),
        name="nsa_q",
    )(x, gmix.reshape(1, D), wq, wg, qn, cos, sin, pool, expand)


def _col_softmax(logits, mask):
    m = jnp.max(jnp.where(mask, logits, NEG_INF), axis=0, keepdims=True)
    e = jnp.where(mask, jnp.exp(logits - m), 0.0)
    l = jnp.sum(e, axis=0, keepdims=True)
    return e, jnp.where(l > 0.0, 1.0 / l, 0.0)


def _nsa_attn_kernel(q_ref, qr_ref, gate_ref, kc_ref, vc_ref, ks_ref, vs_ref, kw_ref, vw_ref, msel_ref,
                     o_ref, memb_ref, *, seq):
    TQ, HPG, dh = Q_TILE, HEADS_PER_GROUP, HEAD_DIM
    NQ = HPG * TQ
    qi = pl.program_id(2)
    s0 = qi * TQ
    heads = lambda ref: [ref[:, hh * dh:(hh + 1) * dh] for hh in range(HPG)]
    q_heads, qr_heads = heads(q_ref), heads(qr_ref)
    qk = lambda keys, qs: jnp.concatenate([_dot_nt(keys, qh) for qh in qs], axis=1)
    tq = lambda shape: s0 + (lax.broadcasted_iota(jnp.int32, shape, 1) & (TQ - 1))
    rows = lambda shape: lax.broadcasted_iota(jnp.int32, shape, 0)

    ncp = kc_ref.shape[0]
    lc = qk(kc_ref[...].astype(BF16), q_heads)
    shape = (ncp, NQ)
    e, inv = _col_softmax(lc, rows(shape) * D_CMP + (L_CMP - 1) <= tq(shape))
    p_cmp = e * inv
    o_cmp = _dot_tn(vc_ref[...].astype(BF16), p_cmp.astype(BF16))

    imp = p_cmp[:, 0:TQ]
    for hh in range(1, HPG):
        imp = imp + p_cmp[:, hh * TQ:(hh + 1) * TQ]
    p_slc = _sel_left(msel_ref[...], imp)
    nsel = p_slc.shape[0]
    shape = (nsel, TQ)
    blk = rows(shape)
    t = tq(shape)
    cur = t >> int(np.log2(L_SEL))
    forced = (blk == 0) | (blk == cur) | (blk == cur - 1)
    score = jnp.where(forced, FORCE_SCORE, p_slc)
    score = jnp.where(blk * L_SEL > t, -1.0, score)
    rank = jnp.zeros(shape, jnp.int32)
    for i in range(nsel):
        si = score[i:i + 1, :]
        beats = jnp.where(si > score, 1, jnp.where(si == score, jnp.where(blk > i, 1, 0), 0))
        rank = rank + beats
    memb_ref[...] = jnp.where(rank < min(N_SEL, nsel), 1.0, 0.0)

    shape = (L_SEL, NQ)
    krow = rows(shape)
    tsel = tq(shape)

    def sel_step(j, carry):
        m, l, acc = carry
        k0 = pl.multiple_of(j * L_SEL, L_SEL)
        logits = qk(ks_ref[pl.ds(k0, L_SEL), :], qr_heads)
        mrow = memb_ref[pl.ds(j, 1), :]
        member = jnp.concatenate([mrow] * HPG, axis=1) > 0.0
        mask = member & (krow + k0 <= tsel)
        m_new = jnp.maximum(m, jnp.max(jnp.where(mask, logits, NEG_INF), axis=0, keepdims=True))
        alpha = jnp.exp(m - m_new)
        p = jnp.where(mask, jnp.exp(logits - m_new), 0.0)
        l = l * alpha + jnp.sum(p, axis=0, keepdims=True)
        acc = acc * alpha + _dot_tn(vs_ref[pl.ds(k0, L_SEL), :], p.astype(BF16))
        return m_new, l, acc

    n_blocks = (s0 + TQ) // L_SEL
    init = (jnp.full((1, NQ), NEG_INF, F32), jnp.zeros((1, NQ), F32), jnp.zeros((dh, NQ), F32))
    _, l_sel, acc_sel = lax.fori_loop(0, n_blocks, sel_step, init)
    o_sel = acc_sel / l_sel

    span = WINDOW + TQ
    start = pl.multiple_of(jnp.maximum(s0 - WINDOW, 0), TQ)
    lw = qk(kw_ref[pl.ds(start, span), :], qr_heads)
    shape = (span, NQ)
    kpos = rows(shape) + start
    twin = tq(shape)
    e, inv = _col_softmax(lw, (kpos <= twin) & (kpos > twin - WINDOW))
    o_win = _dot_tn(vw_ref[pl.ds(start, span), :], e.astype(BF16)) * inv

    outs = []
    for hh in range(HPG):
        sl = slice(hh * TQ, (hh + 1) * TQ)
        g = lambda c: gate_ref[3 * hh + c:3 * hh + c + 1, :]
        o_h = g(0) * o_cmp[:, sl] + g(1) * o_sel[:, sl] + g(2) * o_win[:, sl]
        outs.append(o_h.T)
    o_ref[...] = jnp.concatenate(outs, axis=1)


def _selection_matrix(seq):
    nsel, ncp = seq // L_SEL, seq // D_CMP
    stride = L_SEL // D_CMP
    lpad = L_CMP // D_CMP - 1
    n = np.arange(ncp)[None, :]
    j = np.arange(nsel)[:, None]
    m = (n >= stride * j - lpad) & (n <= stride * j + stride - 1)
    return jnp.asarray(m.astype(np.float32), BF16)


def _nsa_attn(q, qr, gates, k_cmp, v_cmp, k_sel, v_sel, k_win, v_win, *, batch, seq):
    G, HPG, dh = NSA_GROUPS, HEADS_PER_GROUP, HEAD_DIM
    T = batch * seq
    nt = seq // Q_TILE
    gw = G * HPG * 3
    gt = jnp.transpose(gates[:, :gw].reshape(batch, seq, G, HPG * 3), (0, 2, 3, 1))
    gt = jnp.pad(gt, ((0, 0), (0, 0), (0, 16 - HPG * 3), (0, 0)))
    per_group = lambda t: jnp.transpose(t.reshape(batch, seq, G, dh), (0, 2, 1, 3))
    ncp = seq // D_CMP
    msel = _selection_matrix(seq)
    qspec = pl.BlockSpec((Q_TILE, HPG * dh), lambda b, g, i: (b * nt + i, g))
    cspec = pl.BlockSpec((None, ncp, dh), lambda b, g, i: (b * G + g, 0, 0))
    kspec = pl.BlockSpec((None, None, seq, dh), lambda b, g, i: (b, g, 0, 0))
    return pl.pallas_call(
        functools.partial(_nsa_attn_kernel, seq=seq),
        grid=(batch, G, nt),
        in_specs=[qspec, qspec, pl.BlockSpec((None, None, 16, Q_TILE), lambda b, g, i: (b, g, 0, i)),
                  cspec, cspec, kspec, kspec, kspec, kspec,
                  pl.BlockSpec(msel.shape, lambda b, g, i: (0, 0))],
        out_specs=qspec,
        out_shape=jax.ShapeDtypeStruct((T, G * HPG * dh), F32),
        scratch_shapes=[pltpu.VMEM((seq // L_SEL, Q_TILE), F32)],
        compiler_params=_cparams(("parallel", "parallel", "arbitrary")),
        name="nsa_attn",
    )(q, qr, gt, k_cmp, v_cmp, per_group(k_sel), per_group(v_sel), per_group(k_win), per_group(v_win), msel)


def kernel(x, norm_ffn1, ffn1_w_gate, ffn1_w_up, ffn1_w_down, norm_mix, norm_ffn2, ffn2_w_gate, ffn2_w_up, ffn2_w_down, rw_mu, rw_w_rkv, rw_w0, rw_w_lora_a, rw_w_lora_b, rw_a0, rw_a_lora_a, rw_a_lora_b, rw_g_lora_a, rw_g_lora_b, rw_k_k, rw_k_a, rw_r_k, rw_gn_w, rw_gn_b, rw_w_o, kv_norm, kv_w, kv_k_norm, cmp_pos_k, cmp_k_w1, cmp_k_w2, cmp_pos_v, cmp_v_w1, cmp_v_w2, nsa_w_q, nsa_q_norm, nsa_w_o):
    B, S, D = x.shape
    depth = norm_ffn1.shape[0]
    n_rwkv = rw_mu.shape[0]
    assert S % max(Q_TILE, CHUNK) == 0 and S >= WINDOW + Q_TILE and D % 128 == 0
    xt = x.reshape(B * S, D)
    shared = None
    for l in range(depth):
        xt = _ffn(xt, norm_ffn1[l], ffn1_w_gate[l], ffn1_w_up[l], ffn1_w_down[l])
        if l < n_rwkv:
            i = l
            r, lw, k, v, a, g = _rwkv_prep(
                xt, norm_mix[l], rw_mu[i], rw_w_rkv[i], rw_w0[i], rw_w_lora_a[i], rw_w_lora_b[i], rw_a0[i],
                rw_a_lora_a[i], rw_a_lora_b[i], rw_g_lora_a[i], rw_g_lora_b[i], seq=S)
            y = _rwkv_scan(r, lw, k, v, a, rw_k_k[i], rw_k_a[i], rw_r_k[i], rw_gn_w[i], rw_gn_b[i],
                           batch=B, seq=S)
            xt = _out_proj(xt, y, rw_w_o[i], g)
        else:
            i = l - n_rwkv
            q, qr, gates = _nsa_q(xt, norm_mix[l], nsa_w_q[i], nsa_q_norm[i], seq=S)
            o = _nsa_attn(q, qr, gates, *shared, batch=B, seq=S)
            xt = _out_proj(xt, o, nsa_w_o[i])
        xt = _ffn(xt, norm_ffn2[l], ffn2_w_gate[l], ffn2_w_up[l], ffn2_w_down[l])
        if l == n_rwkv - 1:
            kc_tok, vc_tok, k_sel, v_sel, k_win, v_win = _nsa_kv(xt, kv_norm, kv_w, kv_k_norm, seq=S)
            k_cmp, v_cmp = _nsa_cmp(kc_tok, vc_tok, kv_k_norm[0], cmp_pos_k, cmp_k_w1, cmp_k_w2,
                                    cmp_pos_v, cmp_v_w1, cmp_v_w2, batch=B, seq=S)
            shared = (k_cmp, v_cmp, k_sel, v_sel, k_win, v_win)
    return xt.reshape(B, S, D)
```

```python
import functools

import numpy as np
import jax
import jax.numpy as jnp
from jax import lax
from jax.experimental import pallas as pl
from jax.experimental.pallas import tpu as pltpu

F32 = jnp.float32
BF16 = jnp.bfloat16

NORM_EPS = 1e-6
RW_HEAD_DIM = 64
RW_GN_EPS = 64e-5
NSA_HEADS = 16
NSA_GROUPS = 4
HEADS_PER_GROUP = NSA_HEADS // NSA_GROUPS
HEAD_DIM = 64
L_CMP = 32
D_CMP = 16
L_SEL = 64
N_SEL = 16
WINDOW = 512
ROPE_THETA = 10000.0
FORCE_SCORE = 1e6
NEG_INF = -1e30

CHUNK = 64
INV_BLOCK = 16
Q_TILE = 256
VMEM_LIMIT = 56 * 1024 * 1024


def _cparams(sem):
    return pltpu.CompilerParams(dimension_semantics=sem, vmem_limit_bytes=VMEM_LIMIT)


def _dot(a, b):
    return jnp.dot(a, b, preferred_element_type=F32)


def _dot_nt(a, b):
    return lax.dot_general(a, b, (((1,), (1,)), ((), ())), preferred_element_type=F32)


def _dot_tn(a, b):
    return lax.dot_general(a, b, (((0,), (0,)), ((), ())), preferred_element_type=F32)


def _split2(x):
    hi = x.astype(BF16)
    lo = (x - hi.astype(F32)).astype(BF16)
    return hi, lo


def _split3(x):
    hi = x.astype(BF16)
    r1 = x - hi.astype(F32)
    mid = r1.astype(BF16)
    lo = (r1 - mid.astype(F32)).astype(BF16)
    return hi, mid, lo


def _dot3(a, b, dot=_dot):
    ah, al = _split2(a)
    bh, bl = _split2(b)
    return dot(ah, bh) + (dot(ah, bl) + dot(al, bh))


def _sel_right(x, m01):
    x1, x2, x3 = _split3(x)
    return _dot(x1, m01) + (_dot(x2, m01) + _dot(x3, m01))


def _sel_left(m01, x):
    x1, x2, x3 = _split3(x)
    return _dot(m01, x1) + (_dot(m01, x2) + _dot(m01, x3))


def _rms(x, g):
    return x * lax.rsqrt(jnp.mean(x * x, axis=-1, keepdims=True) + NORM_EPS) * g


def _sigmoid(z):
    return 1.0 / (1.0 + jnp.exp(-z))


def _rope(x, cos, sin_signed):
    n = x.shape[-1]
    half = HEAD_DIM // 2
    lane = lax.broadcasted_iota(jnp.int32, x.shape, 1)
    first = (lane & (HEAD_DIM - 1)) < half
    partner = jnp.where(first, pltpu.roll(x, n - half, axis=1), pltpu.roll(x, half, axis=1))
    return x * cos + partner * sin_signed


def _ffn_kernel(x_ref, g_ref, wg_ref, wu_ref, wd_ref, o_ref, h_ref, acc_ref):
    j = pl.program_id(1)

    @pl.when(j == 0)
    def _():
        h_ref[...] = _rms(x_ref[...], g_ref[...]).astype(BF16)
        acc_ref[...] = jnp.zeros_like(acc_ref)

    h = h_ref[...]
    gate = _dot(h, wg_ref[...])
    up = _dot(h, wu_ref[...])
    act = (gate * _sigmoid(gate) * up).astype(BF16)
    acc_ref[...] += _dot(act, wd_ref[...])

    @pl.when(j == pl.num_programs(1) - 1)
    def _():
        o_ref[...] = x_ref[...] + 0.5 * acc_ref[...]


def _ffn(x, g, wg, wu, wd, *, tm=1024, tf=256):
    T, D = x.shape
    F = wg.shape[1]
    tm = min(tm, T)
    return pl.pallas_call(
        _ffn_kernel,
        grid=(T // tm, F // tf),
        in_specs=[
            pl.BlockSpec((tm, D), lambda i, j: (i, 0)),
            pl.BlockSpec((1, D), lambda i, j: (0, 0)),
            pl.BlockSpec((D, tf), lambda i, j: (0, j)),
            pl.BlockSpec((D, tf), lambda i, j: (0, j)),
            pl.BlockSpec((tf, D), lambda i, j: (j, 0)),
        ],
        out_specs=pl.BlockSpec((tm, D), lambda i, j: (i, 0)),
        out_shape=jax.ShapeDtypeStruct((T, D), F32),
        scratch_shapes=[pltpu.VMEM((tm, D), BF16), pltpu.VMEM((tm, D), F32)],
        compiler_params=_cparams(("parallel", "arbitrary")),
        name="ffn",
    )(x, g.reshape(1, D), wg.astype(BF16), wu.astype(BF16), wd.astype(BF16))


def _rwkv_prep_kernel(x_ref, xp_ref, gm_ref, mu_ref, wrkv_ref, w0_ref, wla_ref, wlb_ref, a0_ref,
                      ala_ref, alb_ref, gla_ref, glb_ref,
                      r_ref, lw_ref, k_ref, v_ref, a_ref, g_ref, *, tiles_per_seq):
    i = pl.program_id(0)
    gm = gm_ref[...]
    h = _rms(x_ref[...], gm)
    hp = _rms(xp_ref[...], gm)[7:8, :]
    hp = jnp.where(i % tiles_per_seq == 0, 0.0, hp)
    row = lax.broadcasted_iota(jnp.int32, h.shape, 0)
    prev = jnp.where(row == 0, hp, pltpu.roll(h, 1, axis=0))
    xx = prev - h
    mu = mu_ref[...]
    mix = lambda c: (h + xx * mu[c:c + 1, :]).astype(BF16)
    r_ref[...] = _dot(mix(0), wrkv_ref[0])
    k_ref[...] = _dot(mix(2), wrkv_ref[1])
    v_ref[...] = _dot(mix(3), wrkv_ref[2])
    z = w0_ref[...] + _dot(jnp.tanh(_dot(mix(1), wla_ref[...])).astype(BF16), wlb_ref[...])
    softplus = jnp.maximum(-z, 0.0) + jnp.log(1.0 + jnp.exp(-jnp.abs(z)))
    lw_ref[...] = -jnp.exp(-softplus - 0.5)
    a_ref[...] = _sigmoid(a0_ref[...] + _dot(_dot(mix(4), ala_ref[...]).astype(BF16), alb_ref[...]))
    g_ref[...] = _dot(_sigmoid(_dot(mix(5), gla_ref[...])).astype(BF16), glb_ref[...])


def _rwkv_prep(x, gmix, mu, w_rkv, w0, wla, wlb, a0, ala, alb, gla, glb, *, seq, tm=256):
    T, D = x.shape
    tm = min(tm, seq)
    full = lambda a: pl.BlockSpec(a.shape, lambda i: (0,) * a.ndim)
    row = lambda a: a.reshape(1, D)
    args = [gmix.reshape(1, D), mu, w_rkv.astype(BF16), row(w0), wla.astype(BF16), wlb.astype(BF16),
            row(a0), ala.astype(BF16), alb.astype(BF16), gla.astype(BF16), glb.astype(BF16)]
    out = jax.ShapeDtypeStruct((T, D), F32)
    return pl.pallas_call(
        functools.partial(_rwkv_prep_kernel, tiles_per_seq=seq // tm),
        grid=(T // tm,),
        in_specs=[pl.BlockSpec((tm, D), lambda i: (i, 0)),
                  pl.BlockSpec((8, D), lambda i: (jnp.maximum(i * (tm // 8) - 1, 0), 0))]
                 + [full(a) for a in args],
        out_specs=[pl.BlockSpec((tm, D), lambda i: (i, 0))] * 6,
        out_shape=[out] * 6,
        compiler_params=_cparams(("parallel",)),
        name="rwkv_prep",
    )(x, x, *args)


def _rwkv_chunk(r, lw, k, v, a, kk_p, ka_p, rk_p, gn_w, gn_b, state, consts):
    tri_incl2, tri_strict2, blockdiag, eye, ltri = consts
    N = RW_HEAD_DIM
    kkr = k * kk_p
    kk = kkr / jnp.maximum(jnp.sqrt(jnp.sum(kkr * kkr, axis=-1, keepdims=True)), 1e-12)
    k2 = k * (1.0 + (a - 1.0) * ka_p)
    kb = kk * a

    cum = _sel_left(ltri, lw)
    yield
    cum_last = cum[CHUNK - 1:CHUNK, :]
    einv = jnp.exp(-cum)
    elast = jnp.exp(cum_last - cum)
    a_t = -kk * jnp.exp(cum - lw)
    r_t = r * jnp.exp(cum)
    b_h = kb * einv
    k_h = k2 * einv
    b_l = kb * elast
    k_l = k2 * elast

    d3 = _dot3
    nt = functools.partial(_dot3, dot=_dot_nt)
    tn = functools.partial(_dot3, dot=_dot_tn)
    sc = nt(jnp.concatenate([a_t, r_t], axis=0), jnp.concatenate([b_h, k_h], axis=0))
    yield
    top = jnp.where(tri_strict2, sc[:CHUNK], 0.0)
    bot = jnp.where(tri_incl2, sc[CHUNK:], 0.0)
    m_ab = top[:, :N]
    m_ak = top[:, N:]

    m_d = jnp.where(blockdiag, m_ab, 0.0)
    m_o = m_ab - m_d
    dinv = eye + m_d
    pw = m_d
    w1 = d3(m_ak, v)
    for _ in range(int(np.log2(INV_BLOCK)) - 1):
        pw = d3(pw, pw)
        yield
        dinv = dinv + d3(dinv, pw)
        yield
    n1 = d3(dinv, m_o)
    x = d3(dinv, jnp.concatenate([a_t, w1], axis=1))
    yield
    n2 = d3(n1, n1)
    yield
    x = x + d3(n2, x)
    yield
    x = x + d3(n1, x)
    yield

    stack = jnp.concatenate([x, jnp.concatenate([jnp.zeros_like(v), v], axis=1)], axis=0)
    z1 = d3(bot, stack)
    z2 = tn(jnp.concatenate([b_l, k_l], axis=0), stack)
    yield
    g1 = r_t + z1[:, :N]
    trans = jnp.where(eye > 0, jnp.exp(cum_last), 0.0) + z2[:, :N]
    ys = d3(jnp.concatenate([g1, trans], axis=0), state)
    yield
    y = ys[:CHUNK] + z1[:, N:]
    new_state = ys[CHUNK:] + z2[:, N:]

    mean = jnp.mean(y, axis=-1, keepdims=True)
    dev = y - mean
    var = jnp.mean(dev * dev, axis=-1, keepdims=True)
    yn = dev * lax.rsqrt(var + RW_GN_EPS) * gn_w + gn_b
    bonus = jnp.sum(r * k2 * rk_p, axis=-1, keepdims=True) * v
    return yn + bonus, new_state


def _rwkv_scan_kernel(r_ref, lw_ref, k_ref, v_ref, a_ref, kk_ref, ka_ref, rk_ref, gw_ref, gb_ref, o_ref,
                      state_ref, *, n_chunks, heads):
    N = RW_HEAD_DIM

    @pl.when(pl.program_id(2) == 0)
    def _():
        state_ref[...] = jnp.zeros_like(state_ref)

    row = lax.broadcasted_iota(jnp.int32, (CHUNK, CHUNK), 0)
    col = lax.broadcasted_iota(jnp.int32, (CHUNK, CHUNK), 1)
    row2 = lax.broadcasted_iota(jnp.int32, (CHUNK, 2 * CHUNK), 0)
    col2 = lax.broadcasted_iota(jnp.int32, (CHUNK, 2 * CHUNK), 1) & (CHUNK - 1)
    blockdiag = (row // INV_BLOCK) == (col // INV_BLOCK)
    eye = jnp.where(row == col, 1.0, 0.0).astype(F32)
    ltri = jnp.where(row >= col, 1.0, 0.0).astype(BF16)
    consts = (row2 >= col2, row2 > col2, blockdiag, eye, ltri)

    def body(c, carry):
        t0 = pl.multiple_of(c * CHUNK, CHUNK)
        gens = []
        for hh in range(heads):
            sl = slice(hh * N, (hh + 1) * N)
            ld = lambda ref: ref[pl.ds(t0, CHUNK), sl]
            gens.append(_rwkv_chunk(ld(r_ref), ld(lw_ref), ld(k_ref), ld(v_ref), ld(a_ref),
                                    kk_ref[:, sl], ka_ref[:, sl], rk_ref[:, sl], gw_ref[:, sl], gb_ref[:, sl],
                                    state_ref[hh], consts))
        results = [None] * heads
        while any(res is None for res in results):
            for hh, gen in enumerate(gens):
                if results[hh] is None:
                    try:
                        next(gen)
                    except StopIteration as done:
                        results[hh] = done.value
        for hh, (_, st) in enumerate(results):
            state_ref[hh] = st
        o_ref[pl.ds(t0, CHUNK), :] = jnp.concatenate([y for y, _ in results], axis=1)
        return carry

    lax.fori_loop(0, n_chunks, body, 0)


def _rwkv_scan(r, lw, k, v, a, k_k, k_a, r_k, gn_w, gn_b, *, batch, seq, heads=8, tt=1024):
    T, D = r.shape
    tt = min(tt, seq)
    lanes = heads * RW_HEAD_DIM
    seq_spec = pl.BlockSpec((None, tt, lanes), lambda b, p, t: (b, t, p))
    par_spec = pl.BlockSpec((1, lanes), lambda b, p, t: (0, p))
    as3 = lambda t: t.reshape(batch, seq, D)
    row = lambda t: t.reshape(1, D)
    y = pl.pallas_call(
        functools.partial(_rwkv_scan_kernel, n_chunks=tt // CHUNK, heads=heads),
        grid=(batch, D // lanes, seq // tt),
        in_specs=[seq_spec] * 5 + [par_spec] * 5,
        out_specs=seq_spec,
        out_shape=jax.ShapeDtypeStruct((batch, seq, D), F32),
        scratch_shapes=[pltpu.VMEM((heads, RW_HEAD_DIM, RW_HEAD_DIM), F32)],
        compiler_params=_cparams(("parallel", "parallel", "arbitrary")),
        name="rwkv_scan",
    )(as3(r), as3(lw), as3(k), as3(v), as3(a), row(k_k), row(k_a), row(r_k), row(gn_w), row(gn_b))
    return y.reshape(T, D)


def _gated_out_kernel(x_ref, y_ref, g_ref, w_ref, o_ref):
    o_ref[...] = x_ref[...] + _dot((y_ref[...] * g_ref[...]).astype(BF16), w_ref[...])


def _proj_out_kernel(x_ref, y_ref, w_ref, o_ref):
    o_ref[...] = x_ref[...] + _dot(y_ref[...].astype(BF16), w_ref[...])


def _out_proj(x, y, w, g=None, *, tm=512):
    T, D = x.shape
    tm = min(tm, T)
    tile = pl.BlockSpec((tm, D), lambda i: (i, 0))
    wspec = pl.BlockSpec(w.shape, lambda i: (0, 0))
    if g is None:
        kern, ins, specs = _proj_out_kernel, (x, y, w.astype(BF16)), [tile, tile, wspec]
    else:
        kern, ins, specs = _gated_out_kernel, (x, y, g, w.astype(BF16)), [tile, tile, tile, wspec]
    return pl.pallas_call(
        kern, grid=(T // tm,), in_specs=specs, out_specs=tile,
        out_shape=jax.ShapeDtypeStruct((T, D), F32),
        compiler_params=_cparams(("parallel",)),
        name="out_proj",
    )(*ins)


def _group_rms(x, gain, pool, expand):
    ss = _sel_right(_sel_right(x * x, pool), expand)
    return x * lax.rsqrt(ss * (1.0 / HEAD_DIM) + NORM_EPS) * gain


def _nsa_kv_kernel(x_ref, gn_ref, w_ref, kn_ref, cos_ref, sin_ref, pool_ref, exp_ref,
                   kc_ref, vc_ref, ks_ref, vs_ref, kw_ref, vw_ref):
    h = _rms(x_ref[...], gn_ref[...]).astype(BF16)
    kv = _dot(h, w_ref[...])
    W = NSA_GROUPS * HEAD_DIM
    part = lambda i: kv[:, i * W:(i + 1) * W]
    cos, sin = cos_ref[...], sin_ref[...]
    pool, expand = pool_ref[...], exp_ref[...]
    kc_ref[...] = part(0)
    vc_ref[...] = part(1)
    ks_ref[...] = _rope(_group_rms(part(2), kn_ref[1:2, :], pool, expand), cos, sin).astype(BF16)
    vs_ref[...] = part(3).astype(BF16)
    kw_ref[...] = _rope(_group_rms(part(4), kn_ref[2:3, :], pool, expand), cos, sin).astype(BF16)
    vw_ref[...] = part(5).astype(BF16)


def _pool_matrices(width):
    head = np.arange(width) // HEAD_DIM
    pool = (head[:, None] == np.arange(128)[None, :]).astype(np.float32)
    return jnp.asarray(pool, BF16), jnp.asarray(pool.T, BF16)


def _rope_tables(seq, width):
    half = HEAD_DIM // 2
    inv = ROPE_THETA ** (-jnp.arange(half, dtype=F32) / half)
    ang = jnp.arange(seq).astype(F32)[:, None] * inv[None, :]
    cos, sin = jnp.cos(ang), jnp.sin(ang)
    reps = width // HEAD_DIM
    return (jnp.tile(jnp.concatenate([cos, cos], axis=1), (1, reps)),
            jnp.tile(jnp.concatenate([-sin, sin], axis=1), (1, reps)))


def _nsa_kv(x, kv_norm, kv_w, kv_k_norm, *, seq, tm=512):
    T, D = x.shape
    tm = min(tm, seq)
    W = NSA_GROUPS * HEAD_DIM
    cos, sin = _rope_tables(seq, W)
    pool, expand = _pool_matrices(W)
    kn = jnp.tile(kv_k_norm, (1, NSA_GROUPS))
    tps = seq // tm
    full = lambda a: pl.BlockSpec(a.shape, lambda i: (0,) * a.ndim)
    tab = pl.BlockSpec((tm, W), lambda i: (i % tps, 0))
    out_tile = pl.BlockSpec((tm, W), lambda i: (i, 0))
    w = kv_w.astype(BF16)
    return pl.pallas_call(
        _nsa_kv_kernel,
        grid=(T // tm,),
        in_specs=[pl.BlockSpec((tm, D), lambda i: (i, 0)), pl.BlockSpec((1, D), lambda i: (0, 0)),
                  full(w), full(kn), tab, tab, full(pool), full(expand)],
        out_specs=[out_tile] * 6,
        out_shape=[jax.ShapeDtypeStruct((T, W), F32)] * 2 + [jax.ShapeDtypeStruct((T, W), BF16)] * 4,
        compiler_params=_cparams(("parallel",)),
        name="nsa_kv",
    )(x, kv_norm.reshape(1, D), w, kn, cos, sin, pool, expand)


def _compress(c, pos, w1, w2):
    half = w1.shape[0] // 2
    lo = _dot((c + pos[0:1, :]).astype(BF16), w1[:half, :])
    hi = _dot((c + pos[1:2, :]).astype(BF16), w1[half:, :])
    hid = lo + pltpu.roll(hi, hi.shape[0] - 1, axis=0)
    act = 0.5 * hid * (1.0 + jnp.tanh(np.sqrt(2.0 / np.pi) * (hid + 0.044715 * hid * hid * hid)))
    return _dot(act.astype(BF16), w2)


def _nsa_cmp_kernel(ck_ref, cv_ref, pk_ref, pv_ref, w1k_ref, w2k_ref, w1v_ref, w2v_ref, kn_ref,
                    ko_ref, vo_ref):
    kc = _compress(ck_ref[...], pk_ref[...], w1k_ref[...], w2k_ref[...])
    ko_ref[...] = _rms(kc, kn_ref[...])
    vo_ref[...] = _compress(cv_ref[...], pv_ref[...], w1v_ref[...], w2v_ref[...])


def _nsa_cmp(kc_tok, vc_tok, kn0, pos_k, w1k, w2k, pos_v, w1v, w2v, *, batch, seq):
    G, dh = NSA_GROUPS, HEAD_DIM
    nch = seq // D_CMP
    cw = D_CMP * dh

    def chunks(t):
        t = t.reshape(batch, nch, D_CMP, G, dh)
        return jnp.transpose(t, (0, 3, 1, 2, 4)).reshape(batch * G, nch, cw)

    full = lambda a: pl.BlockSpec(a.shape, lambda i: (0,) * a.ndim)
    blk = pl.BlockSpec((None, nch, cw), lambda i: (i, 0, 0))
    oblk = pl.BlockSpec((None, nch, dh), lambda i: (i, 0, 0))
    args = [pos_k.reshape(2, cw), pos_v.reshape(2, cw), w1k.astype(BF16), w2k.astype(BF16),
            w1v.astype(BF16), w2v.astype(BF16), kn0.reshape(1, dh)]
    return pl.pallas_call(
        _nsa_cmp_kernel,
        grid=(batch * G,),
        in_specs=[blk, blk] + [full(a) for a in args],
        out_specs=[oblk, oblk],
        out_shape=[jax.ShapeDtypeStruct((batch * G, nch, dh), F32)] * 2,
        compiler_params=_cparams(("parallel",)),
        name="nsa_cmp",
    )(chunks(kc_tok), chunks(vc_tok), *args)


def _nsa_q_kernel(x_ref, gn_ref, wq_ref, wg_ref, qn_ref, cos_ref, sin_ref, pool_ref, exp_ref,
                  q_ref, qr_ref, gate_ref):
    h = _rms(x_ref[...], gn_ref[...]).astype(BF16)
    q = _group_rms(_dot(h, wq_ref[...]), qn_ref[...], pool_ref[...], exp_ref[...])
    reps = q.shape[1] // cos_ref.shape[1]
    cos = jnp.concatenate([cos_ref[...]] * reps, axis=1)
    sin = jnp.concatenate([sin_ref[...]] * reps, axis=1)
    scale = HEAD_DIM ** -0.5
    q_ref[...] = (q * scale).astype(BF16)
    qr_ref[...] = (_rope(q, cos, sin) * scale).astype(BF16)
    gate_ref[...] = _sigmoid(_dot(h, wg_ref[...]))


def _nsa_q(x, gmix, w_q, q_norm, *, seq, tm=512):
    T, D = x.shape
    tm = min(tm, seq)
    HD = NSA_HEADS * HEAD_DIM
    cos, sin = _rope_tables(seq, 128)
    pool, expand = _pool_matrices(HD)
    wq = w_q[:, :HD].astype(BF16)
    wg = jnp.pad(w_q[:, HD:], ((0, 0), (0, 128 - 3 * NSA_HEADS))).astype(BF16)
    qn = jnp.tile(q_norm.reshape(1, HEAD_DIM), (1, NSA_HEADS))
    tps = seq // tm
    full = lambda a: pl.BlockSpec(a.shape, lambda i: (0,) * a.ndim)
    tab = pl.BlockSpec((tm, 128), lambda i: (i % tps, 0))
    tile = lambda w: pl.BlockSpec((tm, w), lambda i: (i, 0))
    return pl.pallas_call(
        _nsa_q_kernel,
        grid=(T // tm,),
        in_specs=[tile(D), pl.BlockSpec((1, D), lambda i: (0, 0)), full(wq), full(wg), full(qn),
                  tab, tab, full(pool), full(expand)],
        out_specs=[tile(HD), tile(HD), tile(128)],
        out_shape=[jax.ShapeDtypeStruct((T, HD), BF16)] * 2 + [jax.ShapeDtypeStruct((T, 128), F32)],
        compiler_params=_cparams(("parallel",)),
        name="nsa_q",
    )(x, gmix.reshape(1, D), wq, wg, qn, cos, sin, pool, expand)


def _col_softmax(logits, mask):
    m = jnp.max(jnp.where(mask, logits, NEG_INF), axis=0, keepdims=True)
    e = jnp.where(mask, jnp.exp(logits - m), 0.0)
    l = jnp.sum(e, axis=0, keepdims=True)
    return e, jnp.where(l > 0.0, 1.0 / l, 0.0)


def _nsa_attn_kernel(q_ref, qr_ref, gate_ref, kc_ref, vc_ref, ks_ref, vs_ref, kw_ref, vw_ref, msel_ref,
                     o_ref, bias_ref, *, seq):
    TQ, HPG, dh = Q_TILE, HEADS_PER_GROUP, HEAD_DIM
    NQ = HPG * TQ
    qi = pl.program_id(2)
    s0 = qi * TQ
    heads = lambda ref: [ref[:, hh * dh:(hh + 1) * dh] for hh in range(HPG)]
    q_heads, qr_heads = heads(q_ref), heads(qr_ref)
    qk = lambda keys, qs: jnp.concatenate([_dot_nt(keys, qh) for qh in qs], axis=1)
    tq = lambda shape: s0 + (lax.broadcasted_iota(jnp.int32, shape, 1) & (TQ - 1))
    rows = lambda shape: lax.broadcasted_iota(jnp.int32, shape, 0)

    ncp = kc_ref.shape[0]
    lc = qk(kc_ref[...].astype(BF16), q_heads)
    shape = (ncp, NQ)
    e, inv = _col_softmax(lc, rows(shape) * D_CMP + (L_CMP - 1) <= tq(shape))
    p_cmp = e * inv
    o_cmp = _dot_tn(vc_ref[...].astype(BF16), p_cmp.astype(BF16))

    imp = p_cmp[:, 0:TQ]
    for hh in range(1, HPG):
        imp = imp + p_cmp[:, hh * TQ:(hh + 1) * TQ]
    p_slc = _sel_left(msel_ref[...], imp)
    nsel = p_slc.shape[0]
    shape = (nsel, TQ)
    blk = rows(shape)
    t = tq(shape)
    cur = t >> int(np.log2(L_SEL))
    forced = (blk == 0) | (blk == cur) | (blk == cur - 1)
    score = jnp.where(forced, FORCE_SCORE, p_slc)
    score = jnp.where(blk * L_SEL > t, -1.0, score)
    rank = jnp.zeros(shape, jnp.int32)
    for i in range(nsel):
        si = score[i:i + 1, :]
        beats = jnp.where(si > score, 1, jnp.where(si == score, jnp.where(blk > i, 1, 0), 0))
        rank = rank + beats
    sel_bias = jnp.where(rank < min(N_SEL, nsel), 0.0, NEG_INF)
    bpt = TQ // L_SEL
    for jj in range(nsel // bpt):
        bias_ref[jj] = sel_bias[jj * bpt:(jj + 1) * bpt, :]

    def sel_tile(j, carry, diagonal):
        m, l, acc = carry
        k0 = pl.multiple_of(j * TQ, TQ)
        logits = qk(ks_ref[pl.ds(k0, TQ), :], qr_heads)
        bias = bias_ref[j]
        slabs = []
        for b in range(bpt):
            s = logits[b * L_SEL:(b + 1) * L_SEL, :] + jnp.concatenate([bias[b:b + 1, :]] * HPG, axis=1)
            if diagonal:
                shape = (L_SEL, NQ)
                causal = rows(shape) + b * L_SEL <= (lax.broadcasted_iota(jnp.int32, shape, 1) & (TQ - 1))
                s = jnp.where(causal, s, NEG_INF)
            slabs.append(s)
        s = jnp.concatenate(slabs, axis=0)
        m_new = jnp.maximum(m, jnp.max(s, axis=0, keepdims=True))
        alpha = jnp.exp(m - m_new)
        p = jnp.exp(s - m_new)
        l = l * alpha + jnp.sum(p, axis=0, keepdims=True)
        acc = acc * alpha + _dot_tn(vs_ref[pl.ds(k0, TQ), :], p.astype(BF16))
        return m_new, l, acc

    carry = (jnp.full((1, NQ), NEG_INF, F32), jnp.zeros((1, NQ), F32), jnp.zeros((dh, NQ), F32))
    carry = lax.fori_loop(0, qi, lambda j, c: sel_tile(j, c, False), carry)
    _, l_sel, acc_sel = sel_tile(qi, carry, True)
    o_sel = acc_sel / l_sel

    span = WINDOW + TQ
    start = pl.multiple_of(jnp.maximum(s0 - WINDOW, 0), TQ)
    lw = qk(kw_ref[pl.ds(start, span), :], qr_heads)
    shape = (span, NQ)
    kpos = rows(shape) + start
    twin = tq(shape)
    e, inv = _col_softmax(lw, (kpos <= twin) & (kpos > twin - WINDOW))
    o_win = _dot_tn(vw_ref[pl.ds(start, span), :], e.astype(BF16)) * inv

    outs = []
    for hh in range(HPG):
        sl = slice(hh * TQ, (hh + 1) * TQ)
        g = lambda c: gate_ref[3 * hh + c:3 * hh + c + 1, :]
        o_h = g(0) * o_cmp[:, sl] + g(1) * o_sel[:, sl] + g(2) * o_win[:, sl]
        outs.append(o_h.T)
    o_ref[...] = jnp.concatenate(outs, axis=1)


def _selection_matrix(seq):
    nsel, ncp = seq // L_SEL, seq // D_CMP
    stride = L_SEL // D_CMP
    lpad = L_CMP // D_CMP - 1
    n = np.arange(ncp)[None, :]
    j = np.arange(nsel)[:, None]
    m = (n >= stride * j - lpad) & (n <= stride * j + stride - 1)
    return jnp.asarray(m.astype(np.float32), BF16)


def _nsa_attn(q, qr, gates, k_cmp, v_cmp, k_sel, v_sel, k_win, v_win, *, batch, seq):
    G, HPG, dh = NSA_GROUPS, HEADS_PER_GROUP, HEAD_DIM
    T = batch * seq
    nt = seq // Q_TILE
    gw = G * HPG * 3
    gt = jnp.transpose(gates[:, :gw].reshape(batch, seq, G, HPG * 3), (0, 2, 3, 1))
    gt = jnp.pad(gt, ((0, 0), (0, 0), (0, 16 - HPG * 3), (0, 0)))
    per_group = lambda t: jnp.transpose(t.reshape(batch, seq, G, dh), (0, 2, 1, 3))
    ncp = seq // D_CMP
    msel = _selection_matrix(seq)
    qspec = pl.BlockSpec((Q_TILE, HPG * dh), lambda b, g, i: (b * nt + i, g))
    cspec = pl.BlockSpec((None, ncp, dh), lambda b, g, i: (b * G + g, 0, 0))
    kspec = pl.BlockSpec((None, None, seq, dh), lambda b, g, i: (b, g, 0, 0))
    return pl.pallas_call(
        functools.partial(_nsa_attn_kernel, seq=seq),
        grid=(batch, G, nt),
        in_specs=[qspec, qspec, pl.BlockSpec((None, None, 16, Q_TILE), lambda b, g, i: (b, g, 0, i)),
                  cspec, cspec, kspec, kspec, kspec, kspec,
                  pl.BlockSpec(msel.shape, lambda b, g, i: (0, 0))],
        out_specs=qspec,
        out_shape=jax.ShapeDtypeStruct((T, G * HPG * dh), F32),
        scratch_shapes=[pltpu.VMEM((seq // Q_TILE, Q_TILE // L_SEL, Q_TILE), F32)],
        compiler_params=_cparams(("parallel", "parallel", "arbitrary")),
        name="nsa_attn",
    )(q, qr, gt, k_cmp, v_cmp, per_group(k_sel), per_group(v_sel), per_group(k_win), per_group(v_win), msel)


def kernel(x, norm_ffn1, ffn1_w_gate, ffn1_w_up, ffn1_w_down, norm_mix, norm_ffn2, ffn2_w_gate, ffn2_w_up, ffn2_w_down, rw_mu, rw_w_rkv, rw_w0, rw_w_lora_a, rw_w_lora_b, rw_a0, rw_a_lora_a, rw_a_lora_b, rw_g_lora_a, rw_g_lora_b, rw_k_k, rw_k_a, rw_r_k, rw_gn_w, rw_gn_b, rw_w_o, kv_norm, kv_w, kv_k_norm, cmp_pos_k, cmp_k_w1, cmp_k_w2, cmp_pos_v, cmp_v_w1, cmp_v_w2, nsa_w_q, nsa_q_norm, nsa_w_o):
    B, S, D = x.shape
    depth = norm_ffn1.shape[0]
    n_rwkv = rw_mu.shape[0]
    assert S % max(Q_TILE, CHUNK) == 0 and S >= WINDOW + Q_TILE and D % 128 == 0
    xt = x.reshape(B * S, D)
    shared = None
    for l in range(depth):
        xt = _ffn(xt, norm_ffn1[l], ffn1_w_gate[l], ffn1_w_up[l], ffn1_w_down[l])
        if l < n_rwkv:
            i = l
            r, lw, k, v, a, g = _rwkv_prep(
                xt, norm_mix[l], rw_mu[i], rw_w_rkv[i], rw_w0[i], rw_w_lora_a[i], rw_w_lora_b[i], rw_a0[i],
                rw_a_lora_a[i], rw_a_lora_b[i], rw_g_lora_a[i], rw_g_lora_b[i], seq=S)
            y = _rwkv_scan(r, lw, k, v, a, rw_k_k[i], rw_k_a[i], rw_r_k[i], rw_gn_w[i], rw_gn_b[i],
                           batch=B, seq=S)
            xt = _out_proj(xt, y, rw_w_o[i], g)
        else:
            i = l - n_rwkv
            q, qr, gates = _nsa_q(xt, norm_mix[l], nsa_w_q[i], nsa_q_norm[i], seq=S)
            o = _nsa_attn(q, qr, gates, *shared, batch=B, seq=S)
            xt = _out_proj(xt, o, nsa_w_o[i])
        xt = _ffn(xt, norm_ffn2[l], ffn2_w_gate[l], ffn2_w_up[l], ffn2_w_down[l])
        if l == n_rwkv - 1:
            kc_tok, vc_tok, k_sel, v_sel, k_win, v_win = _nsa_kv(xt, kv_norm, kv_w, kv_k_norm, seq=S)
            k_cmp, v_cmp = _nsa_cmp(kc_tok, vc_tok, kv_k_norm[0], cmp_pos_k, cmp_k_w1, cmp_k_w2,
                                    cmp_pos_v, cmp_v_w1, cmp_v_w2, batch=B, seq=S)
            shared = (k_cmp, v_cmp, k_sel, v_sel, k_win, v_win)
    return xt.reshape(B, S, D)
```

```python
import functools

import numpy as np
import jax
import jax.numpy as jnp
from jax import lax
from jax.experimental import pallas as pl
from jax.experimental.pallas import tpu as pltpu

F32 = jnp.float32
BF16 = jnp.bfloat16

NORM_EPS = 1e-6
RW_HEAD_DIM = 64
RW_GN_EPS = 64e-5
NSA_HEADS = 16
NSA_GROUPS = 4
HEADS_PER_GROUP = NSA_HEADS // NSA_GROUPS
HEAD_DIM = 64
L_CMP = 32
D_CMP = 16
L_SEL = 64
N_SEL = 16
WINDOW = 512
ROPE_THETA = 10000.0
FORCE_SCORE = 1e6
NEG_INF = -1e30

CHUNK = 64
INV_BLOCK = 16
Q_TILE = 256
VMEM_LIMIT = 56 * 1024 * 1024


def _cparams(sem):
    return pltpu.CompilerParams(dimension_semantics=sem, vmem_limit_bytes=VMEM_LIMIT)


def _dot(a, b):
    return jnp.dot(a, b, preferred_element_type=F32)


def _dot_nt(a, b):
    return lax.dot_general(a, b, (((1,), (1,)), ((), ())), preferred_element_type=F32)


def _dot_tn(a, b):
    return lax.dot_general(a, b, (((0,), (0,)), ((), ())), preferred_element_type=F32)


def _split2(x):
    hi = x.astype(BF16)
    lo = (x - hi.astype(F32)).astype(BF16)
    return hi, lo


def _split3(x):
    hi = x.astype(BF16)
    r1 = x - hi.astype(F32)
    mid = r1.astype(BF16)
    lo = (r1 - mid.astype(F32)).astype(BF16)
    return hi, mid, lo


def _dot3(a, b, dot=_dot):
    ah, al = _split2(a)
    bh, bl = _split2(b)
    return dot(ah, bh) + (dot(ah, bl) + dot(al, bh))


def _sel_right(x, m01):
    x1, x2, x3 = _split3(x)
    return _dot(x1, m01) + (_dot(x2, m01) + _dot(x3, m01))


def _sel_left(m01, x):
    x1, x2, x3 = _split3(x)
    return _dot(m01, x1) + (_dot(m01, x2) + _dot(m01, x3))


def _rms(x, g):
    return x * lax.rsqrt(jnp.mean(x * x, axis=-1, keepdims=True) + NORM_EPS) * g


def _sigmoid(z):
    return 1.0 / (1.0 + jnp.exp(-z))


def _rope(x, cos, sin_signed):
    n = x.shape[-1]
    half = HEAD_DIM // 2
    lane = lax.broadcasted_iota(jnp.int32, x.shape, 1)
    first = (lane & (HEAD_DIM - 1)) < half
    partner = jnp.where(first, pltpu.roll(x, n - half, axis=1), pltpu.roll(x, half, axis=1))
    return x * cos + partner * sin_signed


def _ffn_kernel(x_ref, g_ref, wg_ref, wu_ref, wd_ref, o_ref):
    x = x_ref[...]
    h = _rms(x, g_ref[...]).astype(BF16)
    gate = _dot(h, wg_ref[...])
    up = _dot(h, wu_ref[...])
    act = (gate * _sigmoid(gate) * up).astype(BF16)
    o_ref[...] = x + 0.5 * _dot(act, wd_ref[...])


def _resident(shape):
    return pl.BlockSpec(shape, lambda *_: (0,) * len(shape), pipeline_mode=pl.Buffered(1))


def _ffn(x, g, wg, wu, wd, *, tm=512):
    T, D = x.shape
    tm = min(tm, T)
    tile = pl.BlockSpec((tm, D), lambda i: (i, 0))
    return pl.pallas_call(
        _ffn_kernel,
        grid=(T // tm,),
        in_specs=[tile, _resident((1, D)), _resident(wg.shape), _resident(wu.shape), _resident(wd.shape)],
        out_specs=tile,
        out_shape=jax.ShapeDtypeStruct((T, D), F32),
        compiler_params=_cparams(("parallel",)),
        name="ffn",
    )(x, g.reshape(1, D), wg.astype(BF16), wu.astype(BF16), wd.astype(BF16))


def _rwkv_prep_kernel(x_ref, xp_ref, gm_ref, mu_ref, wrkv_ref, w0_ref, wla_ref, wlb_ref, a0_ref,
                      ala_ref, alb_ref, gla_ref, glb_ref,
                      r_ref, lw_ref, k_ref, v_ref, a_ref, g_ref, *, tiles_per_seq):
    i = pl.program_id(0)
    gm = gm_ref[...]
    h = _rms(x_ref[...], gm)
    hp = _rms(xp_ref[...], gm)[7:8, :]
    hp = jnp.where(i % tiles_per_seq == 0, 0.0, hp)
    row = lax.broadcasted_iota(jnp.int32, h.shape, 0)
    prev = jnp.where(row == 0, hp, pltpu.roll(h, 1, axis=0))
    xx = prev - h
    mu = mu_ref[...]
    mix = lambda c: (h + xx * mu[c:c + 1, :]).astype(BF16)
    r_ref[...] = _dot(mix(0), wrkv_ref[0])
    k_ref[...] = _dot(mix(2), wrkv_ref[1])
    v_ref[...] = _dot(mix(3), wrkv_ref[2])
    z = w0_ref[...] + _dot(jnp.tanh(_dot(mix(1), wla_ref[...])).astype(BF16), wlb_ref[...])
    softplus = jnp.maximum(-z, 0.0) + jnp.log(1.0 + jnp.exp(-jnp.abs(z)))
    lw_ref[...] = -jnp.exp(-softplus - 0.5)
    a_ref[...] = _sigmoid(a0_ref[...] + _dot(_dot(mix(4), ala_ref[...]).astype(BF16), alb_ref[...]))
    g_ref[...] = _dot(_sigmoid(_dot(mix(5), gla_ref[...])).astype(BF16), glb_ref[...])


def _rwkv_prep(x, gmix, mu, w_rkv, w0, wla, wlb, a0, ala, alb, gla, glb, *, seq, tm=256):
    T, D = x.shape
    tm = min(tm, seq)
    full = lambda a: pl.BlockSpec(a.shape, lambda i: (0,) * a.ndim)
    row = lambda a: a.reshape(1, D)
    args = [gmix.reshape(1, D), mu, w_rkv.astype(BF16), row(w0), wla.astype(BF16), wlb.astype(BF16),
            row(a0), ala.astype(BF16), alb.astype(BF16), gla.astype(BF16), glb.astype(BF16)]
    out = jax.ShapeDtypeStruct((T, D), F32)
    return pl.pallas_call(
        functools.partial(_rwkv_prep_kernel, tiles_per_seq=seq // tm),
        grid=(T // tm,),
        in_specs=[pl.BlockSpec((tm, D), lambda i: (i, 0)),
                  pl.BlockSpec((8, D), lambda i: (jnp.maximum(i * (tm // 8) - 1, 0), 0))]
                 + [full(a) for a in args],
        out_specs=[pl.BlockSpec((tm, D), lambda i: (i, 0))] * 6,
        out_shape=[out] * 6,
        compiler_params=_cparams(("parallel",)),
        name="rwkv_prep",
    )(x, x, *args)


PAIR = 2 * RW_HEAD_DIM


def _blockdiag(x, masks):
    return jnp.concatenate([x * masks[0], x * masks[1]], axis=0)


def _pair_dot(lhs, rhs_list, masks, passes, transpose_rhs=False):
    dot = _dot_nt if transpose_rhs else _dot
    expand = lambda parts: jnp.concatenate([_blockdiag(p, masks) for p in parts], axis=0 if transpose_rhs else 1)
    if passes == 1:
        out = dot(lhs.astype(BF16), expand([r.astype(BF16) for r in rhs_list]))
    else:
        lh, ll = _split2(lhs)
        splits = [_split2(r) for r in rhs_list]
        rh = expand([s[0] for s in splits])
        rl = expand([s[1] for s in splits])
        out = dot(lh, rh) + (dot(lh, rl) + dot(ll, rh))
    return [out[:, i * PAIR:(i + 1) * PAIR] for i in range(len(rhs_list))]


def _seg_sum(x, is_a):
    zero = jnp.zeros_like(x)
    sa = jnp.sum(jnp.where(is_a, x, zero), axis=-1, keepdims=True)
    sb = jnp.sum(jnp.where(is_a, zero, x), axis=-1, keepdims=True)
    return jnp.where(is_a, sa, sb)


def _rwkv_chunk(r, lw, k, v, a, kk_p, ka_p, rk_p, gn_w, gn_b, state, consts):
    is_a, masks, tri_incl, tri_strict, blockdiag, eye, ltri = consts
    inv_n = 1.0 / RW_HEAD_DIM
    kkr = k * kk_p
    kk = kkr / jnp.maximum(jnp.sqrt(_seg_sum(kkr * kkr, is_a)), 1e-12)
    k2 = k * (1.0 + (a - 1.0) * ka_p)
    kb = kk * a

    cum = _sel_left(ltri, lw)
    yield
    cum_last = cum[CHUNK - 1:CHUNK, :]
    einv = jnp.exp(-cum)
    elast = jnp.exp(cum_last - cum)
    a_t = -kk * jnp.exp(cum - lw)
    r_t = r * jnp.exp(cum)
    b_h = kb * einv
    k_h = k2 * einv
    b_l = kb * elast
    k_l = k2 * elast

    ar = jnp.concatenate([a_t, r_t], axis=0)
    sc_b, sc_k = _pair_dot(ar, [b_h, k_h], masks, 3, transpose_rhs=True)
    yield
    m_ab = jnp.where(tri_strict, sc_b[:CHUNK], 0.0)
    m_rb = jnp.where(tri_incl, sc_b[CHUNK:], 0.0)
    m_ak = jnp.where(tri_strict, sc_k[:CHUNK], 0.0)
    m_rk = jnp.where(tri_incl, sc_k[CHUNK:], 0.0)

    (w1,) = _pair_dot(m_ak, [v], masks, 3)
    d1 = lambda p, qs: _pair_dot(p, qs, masks, 1)
    m_d = jnp.where(blockdiag, m_ab, 0.0)
    m_o = m_ab - m_d
    dinv = eye + m_d
    pw = m_d
    for _ in range(int(np.log2(INV_BLOCK)) - 1):
        (pw,) = d1(pw, [pw])
        yield
        dinv = dinv + d1(dinv, [pw])[0]
        yield
    n1, x1, x2 = d1(dinv, [m_o, a_t, w1])
    yield
    (n2,) = d1(n1, [n1])
    yield
    u1, u2 = d1(n2, [x1, x2])
    x1, x2 = x1 + u1, x2 + u2
    yield
    u1, u2 = d1(n1, [x1, x2])
    x1, x2 = x1 + u1, x2 + u2
    yield

    (z_g,) = _pair_dot(m_rb, [x1], masks, 3)
    g1 = r_t + z_g
    lh, ll = _split2(jnp.concatenate([m_rb, m_rk], axis=1))
    (x2h, x2l), (vh, vl) = _split2(x2), _split2(v)
    sh = jnp.concatenate([_blockdiag(x2h, masks), _blockdiag(vh, masks)], axis=0)
    sl = jnp.concatenate([_blockdiag(x2l, masks), _blockdiag(vl, masks)], axis=0)
    y_c = _dot(lh, sh) + (_dot(lh, sl) + _dot(ll, sh))
    lh, ll = _split2(jnp.concatenate([b_l, k_l], axis=0))
    zero = jnp.zeros_like(v)
    rh, rl = _split2(jnp.concatenate([jnp.concatenate([x1, x2], axis=1),
                                      jnp.concatenate([zero, v], axis=1)], axis=0))
    z2 = _dot_tn(lh, rh) + (_dot_tn(lh, rl) + _dot_tn(ll, rh))
    yield
    diag_blocks = lambda f: jnp.where(is_a, f[:RW_HEAD_DIM], f[RW_HEAD_DIM:])
    trans = jnp.where(eye > 0, jnp.exp(cum_last), 0.0) + diag_blocks(z2[:, :PAIR])
    add = diag_blocks(z2[:, PAIR:])
    (ys,) = _pair_dot(jnp.concatenate([g1, trans], axis=0), [state], masks, 3)
    yield
    y = ys[:CHUNK] + y_c
    new_state = ys[CHUNK:] + add

    mean = _seg_sum(y, is_a) * inv_n
    dev = y - mean
    var = _seg_sum(dev * dev, is_a) * inv_n
    yn = dev * lax.rsqrt(var + RW_GN_EPS) * gn_w + gn_b
    bonus = _seg_sum(r * k2 * rk_p, is_a) * v
    return yn + bonus, new_state


def _rwkv_scan_kernel(r_ref, lw_ref, k_ref, v_ref, a_ref, kk_ref, ka_ref, rk_ref, gw_ref, gb_ref, o_ref,
                      state_ref, *, n_chunks, pairs):
    @pl.when(pl.program_id(2) == 0)
    def _():
        state_ref[...] = jnp.zeros_like(state_ref)

    row = lax.broadcasted_iota(jnp.int32, (CHUNK, PAIR), 0)
    lane = lax.broadcasted_iota(jnp.int32, (CHUNK, PAIR), 1)
    col = lane & (RW_HEAD_DIM - 1)
    is_a = lane < RW_HEAD_DIM
    blockdiag = (row // INV_BLOCK) == (col // INV_BLOCK)
    eye = jnp.where(row == col, 1.0, 0.0).astype(F32)
    row_s = lax.broadcasted_iota(jnp.int32, (CHUNK, CHUNK), 0)
    col_s = lax.broadcasted_iota(jnp.int32, (CHUNK, CHUNK), 1)
    ltri = jnp.where(row_s >= col_s, 1.0, 0.0).astype(BF16)
    masks = (jnp.where(is_a, 1.0, 0.0).astype(BF16), jnp.where(is_a, 0.0, 1.0).astype(BF16))
    consts = (is_a, masks, row >= col, row > col, blockdiag, eye, ltri)

    def body(c, carry):
        t0 = pl.multiple_of(c * CHUNK, CHUNK)
        gens = []
        for p in range(pairs):
            sl = slice(p * PAIR, (p + 1) * PAIR)
            ld = lambda ref: ref[pl.ds(t0, CHUNK), sl]
            gens.append(_rwkv_chunk(ld(r_ref), ld(lw_ref), ld(k_ref), ld(v_ref), ld(a_ref),
                                    kk_ref[:, sl], ka_ref[:, sl], rk_ref[:, sl], gw_ref[:, sl], gb_ref[:, sl],
                                    state_ref[p], consts))
        results = [None] * pairs
        while any(res is None for res in results):
            for p, gen in enumerate(gens):
                if results[p] is None:
                    try:
                        next(gen)
                    except StopIteration as done:
                        results[p] = done.value
        for p, (y, st) in enumerate(results):
            state_ref[p] = st
            o_ref[pl.ds(t0, CHUNK), p * PAIR:(p + 1) * PAIR] = y
        return carry

    lax.fori_loop(0, n_chunks, body, 0)


def _rwkv_scan(r, lw, k, v, a, k_k, k_a, r_k, gn_w, gn_b, *, batch, seq, pairs=8, tt=512):
    T, D = r.shape
    tt = min(tt, seq)
    lanes = pairs * PAIR
    seq_spec = pl.BlockSpec((None, tt, lanes), lambda b, p, t: (b, t, p))
    par_spec = pl.BlockSpec((1, lanes), lambda b, p, t: (0, p))
    as3 = lambda t: t.reshape(batch, seq, D)
    row = lambda t: t.reshape(1, D)
    y = pl.pallas_call(
        functools.partial(_rwkv_scan_kernel, n_chunks=tt // CHUNK, pairs=pairs),
        grid=(batch, D // lanes, seq // tt),
        in_specs=[seq_spec] * 5 + [par_spec] * 5,
        out_specs=seq_spec,
        out_shape=jax.ShapeDtypeStruct((batch, seq, D), F32),
        scratch_shapes=[pltpu.VMEM((pairs, RW_HEAD_DIM, PAIR), F32)],
        compiler_params=_cparams(("parallel", "parallel", "arbitrary")),
        name="rwkv_scan",
    )(as3(r), as3(lw), as3(k), as3(v), as3(a), row(k_k), row(k_a), row(r_k), row(gn_w), row(gn_b))
    return y.reshape(T, D)


def _gated_out_kernel(x_ref, y_ref, g_ref, w_ref, o_ref):
    o_ref[...] = x_ref[...] + _dot((y_ref[...] * g_ref[...]).astype(BF16), w_ref[...])


def _proj_out_kernel(x_ref, y_ref, w_ref, o_ref):
    o_ref[...] = x_ref[...] + _dot(y_ref[...].astype(BF16), w_ref[...])


def _out_proj(x, y, w, g=None, *, tm=512):
    T, D = x.shape
    tm = min(tm, T)
    tile = pl.BlockSpec((tm, D), lambda i: (i, 0))
    wspec = pl.BlockSpec(w.shape, lambda i: (0, 0))
    if g is None:
        kern, ins, specs = _proj_out_kernel, (x, y, w.astype(BF16)), [tile, tile, wspec]
    else:
        kern, ins, specs = _gated_out_kernel, (x, y, g, w.astype(BF16)), [tile, tile, tile, wspec]
    return pl.pallas_call(
        kern, grid=(T // tm,), in_specs=specs, out_specs=tile,
        out_shape=jax.ShapeDtypeStruct((T, D), F32),
        compiler_params=_cparams(("parallel",)),
        name="out_proj",
    )(*ins)


def _group_rms(x, gain, pool, expand):
    ss = _sel_right(_sel_right(x * x, pool), expand)
    return x * lax.rsqrt(ss * (1.0 / HEAD_DIM) + NORM_EPS) * gain


def _nsa_kv_kernel(x_ref, gn_ref, w_ref, kn_ref, cos_ref, sin_ref, pool_ref, exp_ref,
                   kc_ref, vc_ref, ks_ref, vs_ref, kw_ref, vw_ref):
    h = _rms(x_ref[...], gn_ref[...]).astype(BF16)
    kv = _dot(h, w_ref[...])
    W = NSA_GROUPS * HEAD_DIM
    part = lambda i: kv[:, i * W:(i + 1) * W]
    cos, sin = cos_ref[...], sin_ref[...]
    pool, expand = pool_ref[...], exp_ref[...]
    kc_ref[...] = part(0)
    vc_ref[...] = part(1)
    ks_ref[...] = _rope(_group_rms(part(2), kn_ref[1:2, :], pool, expand), cos, sin).astype(BF16)
    vs_ref[...] = part(3).astype(BF16)
    kw_ref[...] = _rope(_group_rms(part(4), kn_ref[2:3, :], pool, expand), cos, sin).astype(BF16)
    vw_ref[...] = part(5).astype(BF16)


def _pool_matrices(width):
    head = np.arange(width) // HEAD_DIM
    pool = (head[:, None] == np.arange(128)[None, :]).astype(np.float32)
    return jnp.asarray(pool, BF16), jnp.asarray(pool.T, BF16)


def _rope_tables(seq, width):
    half = HEAD_DIM // 2
    inv = ROPE_THETA ** (-jnp.arange(half, dtype=F32) / half)
    ang = jnp.arange(seq).astype(F32)[:, None] * inv[None, :]
    cos, sin = jnp.cos(ang), jnp.sin(ang)
    reps = width // HEAD_DIM
    return (jnp.tile(jnp.concatenate([cos, cos], axis=1), (1, reps)),
            jnp.tile(jnp.concatenate([-sin, sin], axis=1), (1, reps)))


def _nsa_kv(x, kv_norm, kv_w, kv_k_norm, *, seq, tm=512):
    T, D = x.shape
    tm = min(tm, seq)
    W = NSA_GROUPS * HEAD_DIM
    cos, sin = _rope_tables(seq, W)
    pool, expand = _pool_matrices(W)
    kn = jnp.tile(kv_k_norm, (1, NSA_GROUPS))
    tps = seq // tm
    full = lambda a: pl.BlockSpec(a.shape, lambda i: (0,) * a.ndim)
    tab = pl.BlockSpec((tm, W), lambda i: (i % tps, 0))
    out_tile = pl.BlockSpec((tm, W), lambda i: (i, 0))
    w = kv_w.astype(BF16)
    return pl.pallas_call(
        _nsa_kv_kernel,
        grid=(T // tm,),
        in_specs=[pl.BlockSpec((tm, D), lambda i: (i, 0)), pl.BlockSpec((1, D), lambda i: (0, 0)),
                  full(w), full(kn), tab, tab, full(pool), full(expand)],
        out_specs=[out_tile] * 6,
        out_shape=[jax.ShapeDtypeStruct((T, W), F32)] * 2 + [jax.ShapeDtypeStruct((T, W), BF16)] * 4,
        compiler_params=_cparams(("parallel",)),
        name="nsa_kv",
    )(x, kv_norm.reshape(1, D), w, kn, cos, sin, pool, expand)


def _compress(c, pos, w1, w2):
    half = w1.shape[0] // 2
    lo = _dot((c + pos[0:1, :]).astype(BF16), w1[:half, :])
    hi = _dot((c + pos[1:2, :]).astype(BF16), w1[half:, :])
    hid = lo + pltpu.roll(hi, hi.shape[0] - 1, axis=0)
    act = 0.5 * hid * (1.0 + jnp.tanh(np.sqrt(2.0 / np.pi) * (hid + 0.044715 * hid * hid * hid)))
    return _dot(act.astype(BF16), w2)


def _nsa_cmp_kernel(ck_ref, cv_ref, pk_ref, pv_ref, w1k_ref, w2k_ref, w1v_ref, w2v_ref, kn_ref,
                    ko_ref, vo_ref):
    kc = _compress(ck_ref[...], pk_ref[...], w1k_ref[...], w2k_ref[...])
    ko_ref[...] = _rms(kc, kn_ref[...])
    vo_ref[...] = _compress(cv_ref[...], pv_ref[...], w1v_ref[...], w2v_ref[...])


def _nsa_cmp(kc_tok, vc_tok, kn0, pos_k, w1k, w2k, pos_v, w1v, w2v, *, batch, seq):
    G, dh = NSA_GROUPS, HEAD_DIM
    nch = seq // D_CMP
    cw = D_CMP * dh

    def chunks(t):
        t = t.reshape(batch, nch, D_CMP, G, dh)
        return jnp.transpose(t, (0, 3, 1, 2, 4)).reshape(batch * G, nch, cw)

    full = lambda a: pl.BlockSpec(a.shape, lambda i: (0,) * a.ndim)
    blk = pl.BlockSpec((None, nch, cw), lambda i: (i, 0, 0))
    oblk = pl.BlockSpec((None, nch, dh), lambda i: (i, 0, 0))
    args = [pos_k.reshape(2, cw), pos_v.reshape(2, cw), w1k.astype(BF16), w2k.astype(BF16),
            w1v.astype(BF16), w2v.astype(BF16), kn0.reshape(1, dh)]
    return pl.pallas_call(
        _nsa_cmp_kernel,
        grid=(batch * G,),
        in_specs=[blk, blk] + [full(a) for a in args],
        out_specs=[oblk, oblk],
        out_shape=[jax.ShapeDtypeStruct((batch * G, nch, dh), F32)] * 2,
        compiler_params=_cparams(("parallel",)),
        name="nsa_cmp",
    )(chunks(kc_tok), chunks(vc_tok), *args)


def _nsa_q_kernel(x_ref, gn_ref, wq_ref, wg_ref, qn_ref, cos_ref, sin_ref, pool_ref, exp_ref,
                  q_ref, qr_ref, gate_ref):
    h = _rms(x_ref[...], gn_ref[...]).astype(BF16)
    q = _group_rms(_dot(h, wq_ref[...]), qn_ref[...], pool_ref[...], exp_ref[...])
    reps = q.shape[1] // cos_ref.shape[1]
    cos = jnp.concatenate([cos_ref[...]] * reps, axis=1)
    sin = jnp.concatenate([sin_ref[...]] * reps, axis=1)
    scale = HEAD_DIM ** -0.5
    q_ref[...] = (q * scale).astype(BF16)
    qr_ref[...] = (_rope(q, cos, sin) * scale).astype(BF16)
    gate_ref[...] = _sigmoid(_dot(h, wg_ref[...]))


def _nsa_q(x, gmix, w_q, q_norm, *, seq, tm=512):
    T, D = x.shape
    tm = min(tm, seq)
    HD = NSA_HEADS * HEAD_DIM
    cos, sin = _rope_tables(seq, 128)
    pool, expand = _pool_matrices(HD)
    wq = w_q[:, :HD].astype(BF16)
    wg = jnp.pad(w_q[:, HD:], ((0, 0), (0, 128 - 3 * NSA_HEADS))).astype(BF16)
    qn = jnp.tile(q_norm.reshape(1, HEAD_DIM), (1, NSA_HEADS))
    tps = seq // tm
    full = lambda a: pl.BlockSpec(a.shape, lambda i: (0,) * a.ndim)
    tab = pl.BlockSpec((tm, 128), lambda i: (i % tps, 0))
    tile = lambda w: pl.BlockSpec((tm, w), lambda i: (i, 0))
    return pl.pallas_call(
        _nsa_q_kernel,
        grid=(T // tm,),
        in_specs=[tile(D), pl.BlockSpec((1, D), lambda i: (0, 0)), full(wq), full(wg), full(qn),
                  tab, tab, full(pool), full(expand)],
        out_specs=[tile(HD), tile(HD), tile(128)],
        out_shape=[jax.ShapeDtypeStruct((T, HD), BF16)] * 2 + [jax.ShapeDtypeStruct((T, 128), F32)],
        compiler_params=_cparams(("parallel",)),
        name="nsa_q",
    )(x, gmix.reshape(1, D), wq, wg, qn, cos, sin, pool, expand)


def _bias_softmax(s):
    e = jnp.exp(s - jnp.max(s, axis=0, keepdims=True))
    return e, jnp.sum(e, axis=0, keepdims=True)


RANK_ROWS = 8


def _topk_bias(score, n_top):
    nsel = score.shape[0]
    groups = [score[lo:lo + RANK_ROWS, :] for lo in range(0, nsel, RANK_ROWS)]
    ranks = [jnp.zeros(g.shape, jnp.int32) for g in groups]
    for i in range(nsel):
        si = score[i:i + 1, :]
        for gi, sg in enumerate(groups):
            lo = gi * RANK_ROWS
            if lo > i:
                beats = si >= sg
            elif lo + RANK_ROWS - 1 < i:
                beats = si > sg
            else:
                later = lax.broadcasted_iota(jnp.int32, sg.shape, 0) + lo > i
                beats = (si > sg) | ((si == sg) & later)
            ranks[gi] = ranks[gi] + jnp.where(beats, 1, 0)
    return [jnp.where(r < n_top, 0.0, NEG_INF) for r in ranks]


def _nsa_attn_kernel(q_ref, qr_ref, gate_ref, kc_ref, vc_ref, ks_ref, vs_ref, kw_ref, vw_ref, msel_ref,
                     wbias_ref, o_ref, bias_ref, logit_ref, *, seq):
    TQ, HPG, dh = Q_TILE, HEADS_PER_GROUP, HEAD_DIM
    NQ = HPG * TQ
    qi = pl.program_id(2)
    s0 = qi * TQ
    heads = lambda ref: [ref[:, hh * dh:(hh + 1) * dh] for hh in range(HPG)]
    q_heads, qr_heads = heads(q_ref), heads(qr_ref)
    qk = lambda keys, qs: jnp.concatenate([_dot_nt(keys, qh) for qh in qs], axis=1)
    over_heads = lambda a: jnp.concatenate([a] * HPG, axis=1)
    rows = lambda shape: lax.broadcasted_iota(jnp.int32, shape, 0)
    cols = lambda shape: lax.broadcasted_iota(jnp.int32, shape, 1)

    ncp = kc_ref.shape[0]
    shape = (ncp, TQ)
    cmp_bias = jnp.where(rows(shape) * D_CMP + (L_CMP - 1) <= s0 + cols(shape), 0.0, NEG_INF)
    e, l = _bias_softmax(qk(kc_ref[...].astype(BF16), q_heads) + over_heads(cmp_bias))
    sees_any = s0 + (cols((1, NQ)) & (TQ - 1)) >= L_CMP - 1
    p_cmp = e * jnp.where(sees_any, 1.0 / l, 0.0)
    o_cmp = _dot_tn(vc_ref[...].astype(BF16), p_cmp.astype(BF16))

    imp = p_cmp[:, 0:TQ]
    for hh in range(1, HPG):
        imp = imp + p_cmp[:, hh * TQ:(hh + 1) * TQ]
    p_slc = _sel_left(msel_ref[...], imp)
    nsel = p_slc.shape[0]
    shape = (nsel, TQ)
    blk = rows(shape)
    t = s0 + cols(shape)
    cur = t >> int(np.log2(L_SEL))
    forced = (blk == 0) | (blk == cur) | (blk == cur - 1)
    score = jnp.where(forced, FORCE_SCORE, p_slc)
    score = jnp.where(blk * L_SEL > t, -1.0, score)
    for gi, group_bias in enumerate(_topk_bias(score, min(N_SEL, nsel))):
        bias_ref[gi] = group_bias

    KT = RANK_ROWS * L_SEL
    last = (qi * TQ) // KT
    shape = (KT, TQ)
    causal_bias = jnp.where(rows(shape) <= cols(shape) + (s0 - last * KT), 0.0, NEG_INF)

    def sel_logits(j, m, diagonal):
        k0 = pl.multiple_of(j * KT, KT)
        logits = qk(ks_ref[pl.ds(k0, KT), :], qr_heads)
        bias = bias_ref[j]
        for b in range(RANK_ROWS):
            blk_rows = slice(b * L_SEL, (b + 1) * L_SEL)
            mask = bias[b:b + 1, :] + causal_bias[blk_rows, :] if diagonal else bias[b:b + 1, :]
            s = logits[blk_rows, :] + over_heads(mask)
            logit_ref[j, blk_rows, :] = s
            m = jnp.maximum(m, jnp.max(s, axis=0, keepdims=True))
        return m

    def sel_accumulate(j, carry):
        l, acc = carry
        k0 = pl.multiple_of(j * KT, KT)
        p = jnp.exp(logit_ref[j] - m_sel)
        l = l + jnp.sum(p, axis=0, keepdims=True)
        acc = acc + _dot_tn(vs_ref[pl.ds(k0, KT), :], p.astype(BF16))
        return l, acc

    m_sel = lax.fori_loop(0, last, lambda j, m: sel_logits(j, m, False), jnp.full((1, NQ), NEG_INF, F32))
    m_sel = sel_logits(last, m_sel, True)
    l_sel, acc_sel = lax.fori_loop(0, last + 1, sel_accumulate,
                                   (jnp.zeros((1, NQ), F32), jnp.zeros((dh, NQ), F32)))
    o_sel = acc_sel / l_sel

    span = WINDOW + TQ
    start = pl.multiple_of(jnp.maximum(s0 - WINDOW, 0), TQ)
    e, l = _bias_softmax(qk(kw_ref[pl.ds(start, span), :], qr_heads) + over_heads(wbias_ref[...]))
    o_win = _dot_tn(vw_ref[pl.ds(start, span), :], e.astype(BF16)) / l

    outs = []
    for hh in range(HPG):
        sl = slice(hh * TQ, (hh + 1) * TQ)
        g = lambda c: gate_ref[3 * hh + c:3 * hh + c + 1, :]
        o_h = g(0) * o_cmp[:, sl] + g(1) * o_sel[:, sl] + g(2) * o_win[:, sl]
        outs.append(o_h.T)
    o_ref[...] = jnp.concatenate(outs, axis=1)


def _selection_matrix(seq):
    nsel, ncp = seq // L_SEL, seq // D_CMP
    stride = L_SEL // D_CMP
    lpad = L_CMP // D_CMP - 1
    n = np.arange(ncp)[None, :]
    j = np.arange(nsel)[:, None]
    m = (n >= stride * j - lpad) & (n <= stride * j + stride - 1)
    return jnp.asarray(m.astype(np.float32), BF16)


def _window_bias():
    r = np.arange(WINDOW + Q_TILE)[:, None]
    c = np.arange(Q_TILE)[None, :]
    offs = [min(i * Q_TILE, WINDOW) for i in range(WINDOW // Q_TILE + 1)]
    masks = [np.where((r <= c + off) & (r > c + off - WINDOW), 0.0, NEG_INF) for off in offs]
    return jnp.asarray(np.stack(masks), F32)


def _nsa_attn(q, qr, gates, k_cmp, v_cmp, k_sel, v_sel, k_win, v_win, *, batch, seq):
    G, HPG, dh = NSA_GROUPS, HEADS_PER_GROUP, HEAD_DIM
    T = batch * seq
    nt = seq // Q_TILE
    gw = G * HPG * 3
    gt = jnp.transpose(gates[:, :gw].reshape(batch, seq, G, HPG * 3), (0, 2, 3, 1))
    gt = jnp.pad(gt, ((0, 0), (0, 0), (0, 16 - HPG * 3), (0, 0)))
    per_group = lambda t: jnp.transpose(t.reshape(batch, seq, G, dh), (0, 2, 1, 3))
    ncp = seq // D_CMP
    msel = _selection_matrix(seq)
    wbias = _window_bias()
    key_tile = RANK_ROWS * L_SEL
    assert seq % key_tile == 0 and key_tile % Q_TILE == 0
    qspec = pl.BlockSpec((Q_TILE, HPG * dh), lambda b, g, i: (b * nt + i, g))
    cspec = pl.BlockSpec((None, ncp, dh), lambda b, g, i: (b * G + g, 0, 0))
    kspec = pl.BlockSpec((None, None, seq, dh), lambda b, g, i: (b, g, 0, 0))
    return pl.pallas_call(
        functools.partial(_nsa_attn_kernel, seq=seq),
        grid=(batch, G, nt),
        in_specs=[qspec, qspec, pl.BlockSpec((None, None, 16, Q_TILE), lambda b, g, i: (b, g, 0, i)),
                  cspec, cspec, kspec, kspec, kspec, kspec,
                  pl.BlockSpec(msel.shape, lambda b, g, i: (0, 0)),
                  pl.BlockSpec((None,) + wbias.shape[1:],
                               lambda b, g, i: (jnp.minimum(i, wbias.shape[0] - 1), 0, 0))],
        out_specs=qspec,
        out_shape=jax.ShapeDtypeStruct((T, G * HPG * dh), F32),
        scratch_shapes=[pltpu.VMEM((seq // key_tile, RANK_ROWS, Q_TILE), F32),
                        pltpu.VMEM((seq // key_tile, key_tile, HPG * Q_TILE), F32)],
        compiler_params=_cparams(("parallel", "parallel", "arbitrary")),
        name="nsa_attn",
    )(q, qr, gt, k_cmp, v_cmp, per_group(k_sel), per_group(v_sel), per_group(k_win), per_group(v_win), msel,
      wbias)


def kernel(x, norm_ffn1, ffn1_w_gate, ffn1_w_up, ffn1_w_down, norm_mix, norm_ffn2, ffn2_w_gate, ffn2_w_up, ffn2_w_down, rw_mu, rw_w_rkv, rw_w0, rw_w_lora_a, rw_w_lora_b, rw_a0, rw_a_lora_a, rw_a_lora_b, rw_g_lora_a, rw_g_lora_b, rw_k_k, rw_k_a, rw_r_k, rw_gn_w, rw_gn_b, rw_w_o, kv_norm, kv_w, kv_k_norm, cmp_pos_k, cmp_k_w1, cmp_k_w2, cmp_pos_v, cmp_v_w1, cmp_v_w2, nsa_w_q, nsa_q_norm, nsa_w_o):
    B, S, D = x.shape
    depth = norm_ffn1.shape[0]
    n_rwkv = rw_mu.shape[0]
    assert S % max(Q_TILE, CHUNK) == 0 and S >= WINDOW + Q_TILE and D % 128 == 0
    xt = x.reshape(B * S, D)
    shared = None
    for l in range(depth):
        xt = _ffn(xt, norm_ffn1[l], ffn1_w_gate[l], ffn1_w_up[l], ffn1_w_down[l])
        if l < n_rwkv:
            i = l
            r, lw, k, v, a, g = _rwkv_prep(
                xt, norm_mix[l], rw_mu[i], rw_w_rkv[i], rw_w0[i], rw_w_lora_a[i], rw_w_lora_b[i], rw_a0[i],
                rw_a_lora_a[i], rw_a_lora_b[i], rw_g_lora_a[i], rw_g_lora_b[i], seq=S)
            y = _rwkv_scan(r, lw, k, v, a, rw_k_k[i], rw_k_a[i], rw_r_k[i], rw_gn_w[i], rw_gn_b[i],
                           batch=B, seq=S)
            xt = _out_proj(xt, y, rw_w_o[i], g)
        else:
            i = l - n_rwkv
            q, qr, gates = _nsa_q(xt, norm_mix[l], nsa_w_q[i], nsa_q_norm[i], seq=S)
            o = _nsa_attn(q, qr, gates, *shared, batch=B, seq=S)
            xt = _out_proj(xt, o, nsa_w_o[i])
        xt = _ffn(xt, norm_ffn2[l], ffn2_w_gate[l], ffn2_w_up[l], ffn2_w_down[l])
        if l == n_rwkv - 1:
            kc_tok, vc_tok, k_sel, v_sel, k_win, v_win = _nsa_kv(xt, kv_norm, kv_w, kv_k_norm, seq=S)
            k_cmp, v_cmp = _nsa_cmp(kc_tok, vc_tok, kv_k_norm[0], cmp_pos_k, cmp_k_w1, cmp_k_w2,
                                    cmp_pos_v, cmp_v_w1, cmp_v_w2, batch=B, seq=S)
            shared = (k_cmp, v_cmp, k_sel, v_sel, k_win, v_win)
    return xt.reshape(B, S, D)
```

```python
import functools

import numpy as np
import jax
import jax.numpy as jnp
from jax import lax
from jax.experimental import pallas as pl
from jax.experimental.pallas import tpu as pltpu

F32 = jnp.float32
BF16 = jnp.bfloat16

NORM_EPS = 1e-6
RW_HEAD_DIM = 64
RW_GN_EPS = 64e-5
NSA_HEADS = 16
NSA_GROUPS = 4
HEADS_PER_GROUP = NSA_HEADS // NSA_GROUPS
HEAD_DIM = 64
L_CMP = 32
D_CMP = 16
L_SEL = 64
N_SEL = 16
WINDOW = 512
ROPE_THETA = 10000.0
FORCE_SCORE = 1e6
NEG_INF = -1e30

CHUNK = 64
INV_BLOCK = 16
Q_TILE = 256
VMEM_LIMIT = 56 * 1024 * 1024


def _cparams(sem):
    return pltpu.CompilerParams(dimension_semantics=sem, vmem_limit_bytes=VMEM_LIMIT)


def _dot(a, b):
    return jnp.dot(a, b, preferred_element_type=F32)


def _dot_nt(a, b):
    return lax.dot_general(a, b, (((1,), (1,)), ((), ())), preferred_element_type=F32)


def _dot_tn(a, b):
    return lax.dot_general(a, b, (((0,), (0,)), ((), ())), preferred_element_type=F32)


def _split3(x):
    hi = x.astype(BF16)
    r1 = x - hi.astype(F32)
    mid = r1.astype(BF16)
    lo = (r1 - mid.astype(F32)).astype(BF16)
    return hi, mid, lo


def _sel_left(m01, x):
    x1, x2, x3 = _split3(x)
    return _dot(m01, x1) + (_dot(m01, x2) + _dot(m01, x3))


def _rms(x, g):
    return x * lax.rsqrt(jnp.mean(x * x, axis=-1, keepdims=True) + NORM_EPS) * g


def _sigmoid(z):
    return 1.0 / (1.0 + jnp.exp(-z))


def _rope(x, cos, sin_signed):
    n = x.shape[-1]
    half = HEAD_DIM // 2
    lane = lax.broadcasted_iota(jnp.int32, x.shape, 1)
    first = (lane & (HEAD_DIM - 1)) < half
    partner = jnp.where(first, pltpu.roll(x, n - half, axis=1), pltpu.roll(x, half, axis=1))
    return x * cos + partner * sin_signed


def _ffn_kernel(*refs, mixer, gated):
    refs = list(refs)
    x = refs.pop(0)[...]
    if mixer:
        y = refs.pop(0)[...]
        if gated:
            y = y * refs.pop(0)[...]
        x = x + _dot(y.astype(BF16), refs.pop(0)[...])
    g_ref, wg_ref, wu_ref, wd_ref, o_ref = refs
    h = _rms(x, g_ref[...]).astype(BF16)
    gate = _dot(h, wg_ref[...])
    up = _dot(h, wu_ref[...])
    act = (gate * _sigmoid(gate) * up).astype(BF16)
    o_ref[...] = x + 0.5 * _dot(act, wd_ref[...])


def _resident(shape):
    return pl.BlockSpec(shape, lambda *_: (0,) * len(shape), pipeline_mode=pl.Buffered(1))


def _ffn(x, g, wg, wu, wd, *, mixer=None, tm=512):
    T, D = x.shape
    tm = min(tm, T)
    tile = pl.BlockSpec((tm, D), lambda i: (i, 0))
    ins, specs = [x], [tile]
    if mixer is not None:
        y, y_gate, w_o = mixer
        ins += [y] + ([y_gate] if y_gate is not None else []) + [w_o.astype(BF16)]
        specs += [tile] * (len(ins) - 2) + [_resident(w_o.shape)]
    ins += [g.reshape(1, D), wg.astype(BF16), wu.astype(BF16), wd.astype(BF16)]
    specs += [_resident((1, D)), _resident(wg.shape), _resident(wu.shape), _resident(wd.shape)]
    return pl.pallas_call(
        functools.partial(_ffn_kernel, mixer=mixer is not None, gated=mixer is not None and mixer[1] is not None),
        grid=(T // tm,),
        in_specs=specs,
        out_specs=tile,
        out_shape=jax.ShapeDtypeStruct((T, D), F32),
        compiler_params=_cparams(("parallel",)),
        name="ffn",
    )(*ins)


def _rwkv_prep_kernel(x_ref, xp_ref, gm_ref, mu_ref, wrkv_ref, w0_ref, wla_ref, wlb_ref, a0_ref,
                      ala_ref, alb_ref, gla_ref, glb_ref,
                      r_ref, lw_ref, k_ref, v_ref, a_ref, g_ref, *, tiles_per_seq):
    i = pl.program_id(0)
    gm = gm_ref[...]
    h = _rms(x_ref[...], gm)
    hp = _rms(xp_ref[...], gm)[7:8, :]
    hp = jnp.where(i % tiles_per_seq == 0, 0.0, hp)
    row = lax.broadcasted_iota(jnp.int32, h.shape, 0)
    prev = jnp.where(row == 0, hp, pltpu.roll(h, 1, axis=0))
    xx = prev - h
    mu = mu_ref[...]
    mix = lambda c: (h + xx * mu[c:c + 1, :]).astype(BF16)
    r_ref[...] = _dot(mix(0), wrkv_ref[0])
    k_ref[...] = _dot(mix(2), wrkv_ref[1])
    v_ref[...] = _dot(mix(3), wrkv_ref[2])
    z = w0_ref[...] + _dot(jnp.tanh(_dot(mix(1), wla_ref[...])).astype(BF16), wlb_ref[...])
    softplus = jnp.maximum(-z, 0.0) + jnp.log(1.0 + jnp.exp(-jnp.abs(z)))
    lw_ref[...] = -jnp.exp(-softplus - 0.5)
    a_ref[...] = _sigmoid(a0_ref[...] + _dot(_dot(mix(4), ala_ref[...]).astype(BF16), alb_ref[...]))
    g_ref[...] = _dot(_sigmoid(_dot(mix(5), gla_ref[...])).astype(BF16), glb_ref[...])


def _rwkv_prep(x, gmix, mu, w_rkv, w0, wla, wlb, a0, ala, alb, gla, glb, *, seq, tm=512):
    T, D = x.shape
    tm = min(tm, seq)
    full = lambda a: _resident(a.shape)
    row = lambda a: a.reshape(1, D)
    args = [gmix.reshape(1, D), mu, w_rkv.astype(BF16), row(w0), wla.astype(BF16), wlb.astype(BF16),
            row(a0), ala.astype(BF16), alb.astype(BF16), gla.astype(BF16), glb.astype(BF16)]
    out = jax.ShapeDtypeStruct((T, D), F32)
    return pl.pallas_call(
        functools.partial(_rwkv_prep_kernel, tiles_per_seq=seq // tm),
        grid=(T // tm,),
        in_specs=[pl.BlockSpec((tm, D), lambda i: (i, 0)),
                  pl.BlockSpec((8, D), lambda i: (jnp.maximum(i * (tm // 8) - 1, 0), 0))]
                 + [full(a) for a in args],
        out_specs=[pl.BlockSpec((tm, D), lambda i: (i, 0))] * 6,
        out_shape=[out] * 6,
        compiler_params=_cparams(("parallel",)),
        name="rwkv_prep",
    )(x, x, *args)


PAIR = 2 * RW_HEAD_DIM


def _blockdiag(x, masks):
    return jnp.concatenate([x * masks[0], x * masks[1]], axis=0)


def _pair_dot(lhs, rhs_list, masks, transpose_rhs=False):
    dot = _dot_nt if transpose_rhs else _dot
    rhs = jnp.concatenate([_blockdiag(r.astype(BF16), masks) for r in rhs_list], axis=0 if transpose_rhs else 1)
    out = dot(lhs.astype(BF16), rhs)
    return [out[:, i * PAIR:(i + 1) * PAIR] for i in range(len(rhs_list))]


def _seg_sum(x, is_a):
    zero = jnp.zeros_like(x)
    sa = jnp.sum(jnp.where(is_a, x, zero), axis=-1, keepdims=True)
    sb = jnp.sum(jnp.where(is_a, zero, x), axis=-1, keepdims=True)
    return jnp.where(is_a, sa, sb)


def _rwkv_chunk(r, lw, k, v, a, kk_p, ka_p, rk_p, gn_w, gn_b, state, consts):
    is_a, masks, tri_incl, tri_strict, blockdiag, eye, ltri = consts
    inv_n = 1.0 / RW_HEAD_DIM
    kkr = k * kk_p
    kk = kkr / jnp.maximum(jnp.sqrt(_seg_sum(kkr * kkr, is_a)), 1e-12)
    k2 = k * (1.0 + (a - 1.0) * ka_p)
    kb = kk * a

    cum = _sel_left(ltri, lw)
    yield
    cum_last = cum[CHUNK - 1:CHUNK, :]
    einv = jnp.exp(-cum)
    elast = jnp.exp(cum_last - cum)
    a_t = -kk * jnp.exp(cum - lw)
    r_t = r * jnp.exp(cum)
    b_h = kb * einv
    k_h = k2 * einv
    b_l = kb * elast
    k_l = k2 * elast

    ar = jnp.concatenate([a_t, r_t], axis=0)
    d1 = lambda p, qs: _pair_dot(p, qs, masks)
    sc_b, sc_k = _pair_dot(ar, [b_h, k_h], masks, transpose_rhs=True)
    yield
    m_ab = jnp.where(tri_strict, sc_b[:CHUNK], 0.0)
    m_rb = jnp.where(tri_incl, sc_b[CHUNK:], 0.0)
    m_ak = jnp.where(tri_strict, sc_k[:CHUNK], 0.0)
    m_rk = jnp.where(tri_incl, sc_k[CHUNK:], 0.0)

    (w1,) = d1(m_ak, [v])
    m_d = jnp.where(blockdiag, m_ab, 0.0)
    m_o = m_ab - m_d
    dinv = eye + m_d
    pw = m_d
    for _ in range(int(np.log2(INV_BLOCK)) - 1):
        (pw,) = d1(pw, [pw])
        yield
        dinv = dinv + d1(dinv, [pw])[0]
        yield
    n1, x1, x2 = d1(dinv, [m_o, a_t, w1])
    yield
    (n2,) = d1(n1, [n1])
    yield
    u1, u2 = d1(n2, [x1, x2])
    x1, x2 = x1 + u1, x2 + u2
    yield
    u1, u2 = d1(n1, [x1, x2])
    x1, x2 = x1 + u1, x2 + u2
    yield

    (z_g,) = d1(m_rb, [x1])
    g1 = r_t + z_g
    x2b, vb = x2.astype(BF16), v.astype(BF16)
    y_c = _dot(jnp.concatenate([m_rb, m_rk], axis=1).astype(BF16),
               jnp.concatenate([_blockdiag(x2b, masks), _blockdiag(vb, masks)], axis=0))
    z2 = _dot_tn(jnp.concatenate([b_l, k_l], axis=0).astype(BF16),
                 jnp.concatenate([jnp.concatenate([x1.astype(BF16), x2b], axis=1),
                                  jnp.concatenate([jnp.zeros_like(vb), vb], axis=1)], axis=0))
    yield
    diag_blocks = lambda f: jnp.where(is_a, f[:RW_HEAD_DIM], f[RW_HEAD_DIM:])
    trans = jnp.where(eye > 0, jnp.exp(cum_last), 0.0) + diag_blocks(z2[:, :PAIR])
    add = diag_blocks(z2[:, PAIR:])
    (ys,) = d1(jnp.concatenate([g1, trans], axis=0), [state])
    yield
    y = ys[:CHUNK] + y_c
    new_state = ys[CHUNK:] + add

    mean = _seg_sum(y, is_a) * inv_n
    dev = y - mean
    var = _seg_sum(dev * dev, is_a) * inv_n
    yn = dev * lax.rsqrt(var + RW_GN_EPS) * gn_w + gn_b
    bonus = _seg_sum(r * k2 * rk_p, is_a) * v
    return yn + bonus, new_state


def _rwkv_scan_kernel(r_ref, lw_ref, k_ref, v_ref, a_ref, kk_ref, ka_ref, rk_ref, gw_ref, gb_ref, o_ref,
                      state_ref, *, n_chunks, pairs):
    @pl.when(pl.program_id(2) == 0)
    def _():
        state_ref[...] = jnp.zeros_like(state_ref)

    row = lax.broadcasted_iota(jnp.int32, (CHUNK, PAIR), 0)
    lane = lax.broadcasted_iota(jnp.int32, (CHUNK, PAIR), 1)
    col = lane & (RW_HEAD_DIM - 1)
    is_a = lane < RW_HEAD_DIM
    blockdiag = (row // INV_BLOCK) == (col // INV_BLOCK)
    eye = jnp.where(row == col, 1.0, 0.0).astype(F32)
    row_s = lax.broadcasted_iota(jnp.int32, (CHUNK, CHUNK), 0)
    col_s = lax.broadcasted_iota(jnp.int32, (CHUNK, CHUNK), 1)
    ltri = jnp.where(row_s >= col_s, 1.0, 0.0).astype(BF16)
    masks = (jnp.where(is_a, 1.0, 0.0).astype(BF16), jnp.where(is_a, 0.0, 1.0).astype(BF16))
    consts = (is_a, masks, row >= col, row > col, blockdiag, eye, ltri)

    def body(c, carry):
        t0 = pl.multiple_of(c * CHUNK, CHUNK)
        gens = []
        for p in range(pairs):
            sl = slice(p * PAIR, (p + 1) * PAIR)
            ld = lambda ref: ref[pl.ds(t0, CHUNK), sl]
            gens.append(_rwkv_chunk(ld(r_ref), ld(lw_ref), ld(k_ref), ld(v_ref), ld(a_ref),
                                    kk_ref[:, sl], ka_ref[:, sl], rk_ref[:, sl], gw_ref[:, sl], gb_ref[:, sl],
                                    state_ref[p], consts))
        results = [None] * pairs
        while any(res is None for res in results):
            for p, gen in enumerate(gens):
                if results[p] is None:
                    try:
                        next(gen)
                    except StopIteration as done:
                        results[p] = done.value
        for p, (y, st) in enumerate(results):
            state_ref[p] = st
            o_ref[pl.ds(t0, CHUNK), p * PAIR:(p + 1) * PAIR] = y
        return carry

    lax.fori_loop(0, n_chunks, body, 0)


def _rwkv_scan(r, lw, k, v, a, k_k, k_a, r_k, gn_w, gn_b, *, batch, seq, pairs=8, tt=512):
    T, D = r.shape
    tt = min(tt, seq)
    lanes = pairs * PAIR
    seq_spec = pl.BlockSpec((None, tt, lanes), lambda b, p, t: (b, t, p))
    par_spec = pl.BlockSpec((1, lanes), lambda b, p, t: (0, p))
    as3 = lambda t: t.reshape(batch, seq, D)
    row = lambda t: t.reshape(1, D)
    y = pl.pallas_call(
        functools.partial(_rwkv_scan_kernel, n_chunks=tt // CHUNK, pairs=pairs),
        grid=(batch, D // lanes, seq // tt),
        in_specs=[seq_spec] * 5 + [par_spec] * 5,
        out_specs=seq_spec,
        out_shape=jax.ShapeDtypeStruct((batch, seq, D), F32),
        scratch_shapes=[pltpu.VMEM((pairs, RW_HEAD_DIM, PAIR), F32)],
        compiler_params=_cparams(("parallel", "parallel", "arbitrary")),
        name="rwkv_scan",
    )(as3(r), as3(lw), as3(k), as3(v), as3(a), row(k_k), row(k_a), row(r_k), row(gn_w), row(gn_b))
    return y.reshape(T, D)


def _group_rms(x, gain, pool, expand):
    pooled = _dot((x * x).astype(BF16), pool)
    inv = lax.rsqrt(pooled * (1.0 / HEAD_DIM) + NORM_EPS)
    hi = inv.astype(BF16)
    lo = (inv - hi.astype(F32)).astype(BF16)
    return x * (_dot(hi, expand) + _dot(lo, expand)) * gain


def _nsa_kv_kernel(x_ref, gn_ref, w_ref, kn_ref, cos_ref, sin_ref, pool_ref, exp_ref,
                   kc_ref, vc_ref, ks_ref, vs_ref, kw_ref, vw_ref):
    h = _rms(x_ref[...], gn_ref[...]).astype(BF16)
    kv = _dot(h, w_ref[...])
    W = NSA_GROUPS * HEAD_DIM
    part = lambda i: kv[:, i * W:(i + 1) * W]
    cos, sin = cos_ref[...], sin_ref[...]
    pool, expand = pool_ref[...], exp_ref[...]

    def put(ref, val):
        for g in range(NSA_GROUPS):
            ref[g] = val[:, g * HEAD_DIM:(g + 1) * HEAD_DIM].astype(ref.dtype)

    put(kc_ref, part(0))
    put(vc_ref, part(1))
    put(ks_ref, _rope(_group_rms(part(2), kn_ref[1:2, :], pool, expand), cos, sin))
    put(vs_ref, part(3))
    put(kw_ref, _rope(_group_rms(part(4), kn_ref[2:3, :], pool, expand), cos, sin))
    put(vw_ref, part(5))


def _pool_matrices(width):
    head = np.arange(width) // HEAD_DIM
    pool = (head[:, None] == np.arange(128)[None, :]).astype(np.float32)
    return jnp.asarray(pool, BF16), jnp.asarray(pool.T, BF16)


def _rope_tables(seq, width):
    half = HEAD_DIM // 2
    inv = ROPE_THETA ** (-jnp.arange(half, dtype=F32) / half)
    ang = jnp.arange(seq).astype(F32)[:, None] * inv[None, :]
    cos, sin = jnp.cos(ang), jnp.sin(ang)
    reps = width // HEAD_DIM
    return (jnp.tile(jnp.concatenate([cos, cos], axis=1), (1, reps)),
            jnp.tile(jnp.concatenate([-sin, sin], axis=1), (1, reps)))


def _nsa_kv(x, kv_norm, kv_w, kv_k_norm, *, batch, seq, tm=512):
    T, D = x.shape
    tm = min(tm, seq)
    G, dh = NSA_GROUPS, HEAD_DIM
    W = G * dh
    cos, sin = _rope_tables(seq, W)
    pool, expand = _pool_matrices(W)
    kn = jnp.tile(kv_k_norm, (1, G))
    tps = seq // tm
    full = lambda a: pl.BlockSpec(a.shape, lambda b, t: (0,) * a.ndim)
    tab = pl.BlockSpec((tm, W), lambda b, t: (t, 0))
    out_tile = pl.BlockSpec((None, G, tm, dh), lambda b, t: (b, 0, t, 0))
    w = kv_w.astype(BF16)
    shape = (batch, G, seq, dh)
    return pl.pallas_call(
        _nsa_kv_kernel,
        grid=(batch, tps),
        in_specs=[pl.BlockSpec((tm, D), lambda b, t: (b * tps + t, 0)), pl.BlockSpec((1, D), lambda b, t: (0, 0)),
                  full(w), full(kn), tab, tab, full(pool), full(expand)],
        out_specs=[out_tile] * 6,
        out_shape=[jax.ShapeDtypeStruct(shape, F32)] * 2 + [jax.ShapeDtypeStruct(shape, BF16)] * 4,
        compiler_params=_cparams(("parallel", "parallel")),
        name="nsa_kv",
    )(x, kv_norm.reshape(1, D), w, kn, cos, sin, pool, expand)


def _compress(tok_ref, pos_ref, w1_ref, w2_ref):
    nch = tok_ref.shape[0] // D_CMP
    dh = tok_ref.shape[1]
    lo = hi = None
    for l in range(D_CMP):
        rows = tok_ref[pl.ds(l, nch, stride=D_CMP), :]
        for half in range(L_CMP // D_CMP):
            p = half * D_CMP + l
            term = _dot((rows + pos_ref[p:p + 1, :]).astype(BF16), w1_ref[p * dh:(p + 1) * dh, :])
            if half == 0:
                lo = term if lo is None else lo + term
            else:
                hi = term if hi is None else hi + term
    hid = lo + pltpu.roll(hi, nch - 1, axis=0)
    act = 0.5 * hid * (1.0 + jnp.tanh(np.sqrt(2.0 / np.pi) * (hid + 0.044715 * hid * hid * hid)))
    return _dot(act.astype(BF16), w2_ref[...])


def _nsa_cmp_kernel(ck_ref, cv_ref, pk_ref, pv_ref, w1k_ref, w2k_ref, w1v_ref, w2v_ref, kn_ref,
                    ko_ref, vo_ref):
    ko_ref[...] = _rms(_compress(ck_ref, pk_ref, w1k_ref, w2k_ref), kn_ref[...])
    vo_ref[...] = _compress(cv_ref, pv_ref, w1v_ref, w2v_ref)


def _nsa_cmp(kc_tok, vc_tok, kn0, pos_k, w1k, w2k, pos_v, w1v, w2v, *, batch, seq):
    G, dh = NSA_GROUPS, HEAD_DIM
    assert L_CMP == 2 * D_CMP
    nch = seq // D_CMP
    full = lambda a: pl.BlockSpec(a.shape, lambda b, g: (0,) * a.ndim)
    blk = pl.BlockSpec((None, None, seq, dh), lambda b, g: (b, g, 0, 0))
    oblk = pl.BlockSpec((None, nch, dh), lambda b, g: (b * G + g, 0, 0))
    args = [pos_k, pos_v, w1k.astype(BF16), w2k.astype(BF16), w1v.astype(BF16), w2v.astype(BF16),
            kn0.reshape(1, dh)]
    return pl.pallas_call(
        _nsa_cmp_kernel,
        grid=(batch, G),
        in_specs=[blk, blk] + [full(a) for a in args],
        out_specs=[oblk, oblk],
        out_shape=[jax.ShapeDtypeStruct((batch * G, nch, dh), F32)] * 2,
        compiler_params=_cparams(("parallel", "parallel")),
        name="nsa_cmp",
    )(kc_tok, vc_tok, *args)


def _nsa_q_kernel(x_ref, gn_ref, wq_ref, wg_ref, qn_ref, cos_ref, sin_ref, pool_ref, exp_ref,
                  q_ref, qr_ref, gate_ref):
    h = _rms(x_ref[...], gn_ref[...]).astype(BF16)
    q = _group_rms(_dot(h, wq_ref[...]), qn_ref[...], pool_ref[...], exp_ref[...])
    reps = q.shape[1] // cos_ref.shape[1]
    cos = jnp.concatenate([cos_ref[...]] * reps, axis=1)
    sin = jnp.concatenate([sin_ref[...]] * reps, axis=1)
    scale = HEAD_DIM ** -0.5
    q_ref[...] = (q * scale).astype(BF16)
    qr_ref[...] = (_rope(q, cos, sin) * scale).astype(BF16)
    gate_ref[...] = _sigmoid(_dot(h, wg_ref[...]))


def _nsa_q(x, gmix, w_q, q_norm, *, seq, tm=512):
    T, D = x.shape
    tm = min(tm, seq)
    HD = NSA_HEADS * HEAD_DIM
    cos, sin = _rope_tables(seq, 128)
    pool, expand = _pool_matrices(HD)
    wq = w_q[:, :HD].astype(BF16)
    wg = jnp.pad(w_q[:, HD:], ((0, 0), (0, 128 - 3 * NSA_HEADS))).astype(BF16)
    qn = jnp.tile(q_norm.reshape(1, HEAD_DIM), (1, NSA_HEADS))
    tps = seq // tm
    full = lambda a: pl.BlockSpec(a.shape, lambda i: (0,) * a.ndim)
    tab = pl.BlockSpec((tm, 128), lambda i: (i % tps, 0))
    tile = lambda w: pl.BlockSpec((tm, w), lambda i: (i, 0))
    return pl.pallas_call(
        _nsa_q_kernel,
        grid=(T // tm,),
        in_specs=[tile(D), pl.BlockSpec((1, D), lambda i: (0, 0)), full(wq), full(wg), full(qn),
                  tab, tab, full(pool), full(expand)],
        out_specs=[tile(HD), tile(HD), tile(128)],
        out_shape=[jax.ShapeDtypeStruct((T, HD), BF16)] * 2 + [jax.ShapeDtypeStruct((T, 128), F32)],
        compiler_params=_cparams(("parallel",)),
        name="nsa_q",
    )(x, gmix.reshape(1, D), wq, wg, qn, cos, sin, pool, expand)


RANK_ROWS = 8
MAX_SAFE_SHIFT = 30.0


def _topk_bias(score, n_top):
    nsel = score.shape[0]
    groups = [score[lo:lo + RANK_ROWS, :] for lo in range(0, nsel, RANK_ROWS)]
    ranks = [jnp.zeros(g.shape, jnp.int32) for g in groups]
    for i in range(nsel):
        si = score[i:i + 1, :]
        for gi, sg in enumerate(groups):
            lo = gi * RANK_ROWS
            if lo > i:
                beats = si >= sg
            elif lo + RANK_ROWS - 1 < i:
                beats = si > sg
            else:
                later = lax.broadcasted_iota(jnp.int32, sg.shape, 0) + lo > i
                beats = (si > sg) | ((si == sg) & later)
            ranks[gi] = ranks[gi] + jnp.where(beats, 1, 0)
    return [jnp.where(r < n_top, 0.0, NEG_INF) for r in ranks]


def _nsa_attn_kernel(*refs):
    bound_ref = refs[11]
    safe = ((bound_ref[0] <= MAX_SAFE_SHIFT) & (bound_ref[1] <= MAX_SAFE_SHIFT)
            & (bound_ref[2] <= MAX_SAFE_SHIFT))
    pl.when(safe)(functools.partial(_nsa_attn_step, *refs, use_bound=True))
    pl.when(jnp.logical_not(safe))(functools.partial(_nsa_attn_step, *refs, use_bound=False))


def _nsa_attn_step(q_ref, qr_ref, gate_ref, kc_ref, vc_ref, ks_ref, vs_ref, kw_ref, vw_ref, msel_ref,
                   wbias_ref, bound_ref, o_ref, bias_ref, *, use_bound):
    TQ, HPG, dh = Q_TILE, HEADS_PER_GROUP, HEAD_DIM
    NQ = HPG * TQ
    qi = pl.program_id(2)
    s0 = qi * TQ
    heads = lambda ref: [ref[:, hh * dh:(hh + 1) * dh] for hh in range(HPG)]
    q_heads, qr_heads = heads(q_ref), heads(qr_ref)
    qk = lambda keys, qs: jnp.concatenate([_dot_nt(keys, qh) for qh in qs], axis=1)
    over_heads = lambda a: jnp.concatenate([a] * HPG, axis=1)
    rows = lambda shape: lax.broadcasted_iota(jnp.int32, shape, 0)
    cols = lambda shape: lax.broadcasted_iota(jnp.int32, shape, 1)

    def softmax_then(masked_logits, bound, finish):
        if use_bound:
            e = jnp.exp(masked_logits(bound))
        else:
            s = masked_logits(0.0)
            e = jnp.exp(s - jnp.max(s, axis=0, keepdims=True))
        return finish(e, jnp.sum(e, axis=0, keepdims=True))

    ncp = kc_ref.shape[0]
    shape = (ncp, TQ)
    cmp_bias = jnp.where(rows(shape) * D_CMP + (L_CMP - 1) <= s0 + cols(shape), 0.0, NEG_INF)
    sees_any = s0 + (cols((1, NQ)) & (TQ - 1)) >= L_CMP - 1

    def cmp_finish(e, l):
        p = e * jnp.where(sees_any, 1.0 / l, 0.0)
        imp = p[:, 0:TQ]
        for hh in range(1, HPG):
            imp = imp + p[:, hh * TQ:(hh + 1) * TQ]
        return _dot_tn(vc_ref[...].astype(BF16), p.astype(BF16)), imp

    o_cmp, imp = softmax_then(lambda shift: qk(kc_ref[...].astype(BF16), q_heads) + over_heads(cmp_bias - shift),
                              bound_ref[0], cmp_finish)

    p_slc = _sel_left(msel_ref[...], imp)
    nsel = p_slc.shape[0]
    shape = (nsel, TQ)
    blk = rows(shape)
    t = s0 + cols(shape)
    cur = t >> int(np.log2(L_SEL))
    forced = (blk == 0) | (blk == cur) | (blk == cur - 1)
    score = jnp.where(forced, FORCE_SCORE, p_slc)
    score = jnp.where(blk * L_SEL > t, -1.0, score)
    for gi, group_bias in enumerate(_topk_bias(score, min(N_SEL, nsel))):
        bias_ref[gi] = group_bias

    KT = RANK_ROWS * L_SEL
    last = (qi * TQ) // KT
    shape = (KT, TQ)
    causal_bias = jnp.where(rows(shape) <= cols(shape) + (s0 - last * KT), 0.0, NEG_INF)

    def sel_slabs(j, diagonal, shift):
        k0 = pl.multiple_of(j * KT, KT)
        logits = qk(ks_ref[pl.ds(k0, KT), :], qr_heads)
        bias = bias_ref[j]
        for b in range(RANK_ROWS):
            blk_rows = slice(b * L_SEL, (b + 1) * L_SEL)
            mask = bias[b:b + 1, :] + causal_bias[blk_rows, :] if diagonal else bias[b:b + 1, :]
            yield logits[blk_rows, :] + (over_heads(mask) - shift)

    def sel_max(j, m, diagonal):
        for s in sel_slabs(j, diagonal, 0.0):
            m = jnp.maximum(m, jnp.max(s, axis=0, keepdims=True))
        return m

    def true_max():
        m = lax.fori_loop(0, last, lambda j, m: sel_max(j, m, False), jnp.full((1, NQ), NEG_INF, F32))
        return sel_max(last, m, True)

    m_sel = jnp.full((1, NQ), bound_ref[1], F32) if use_bound else true_max()

    def sel_accumulate(j, carry, diagonal):
        l, acc = carry
        k0 = pl.multiple_of(j * KT, KT)
        ps = []
        for s in sel_slabs(j, diagonal, m_sel):
            p = jnp.exp(s)
            l = l + jnp.sum(p.reshape(L_SEL // 8, 8, NQ), axis=0)
            ps.append(p.astype(BF16))
        acc = acc + _dot_tn(vs_ref[pl.ds(k0, KT), :], jnp.concatenate(ps, axis=0))
        return l, acc

    carry = (jnp.zeros((8, NQ), F32), jnp.zeros((dh, NQ), F32))
    carry = lax.fori_loop(0, last, lambda j, c: sel_accumulate(j, c, False), carry)
    l_sel, acc_sel = sel_accumulate(last, carry, True)
    o_sel = acc_sel / jnp.sum(l_sel, axis=0, keepdims=True)

    span = WINDOW + TQ
    start = pl.multiple_of(jnp.maximum(s0 - WINDOW, 0), TQ)
    o_win = softmax_then(
        lambda shift: qk(kw_ref[pl.ds(start, span), :], qr_heads) + over_heads(wbias_ref[...] - shift),
        bound_ref[2], lambda e, l: _dot_tn(vw_ref[pl.ds(start, span), :], e.astype(BF16)) / l)

    outs = []
    for hh in range(HPG):
        sl = slice(hh * TQ, (hh + 1) * TQ)
        g = lambda c: gate_ref[3 * hh + c:3 * hh + c + 1, :]
        o_h = g(0) * o_cmp[:, sl] + g(1) * o_sel[:, sl] + g(2) * o_win[:, sl]
        outs.append(o_h.T)
    o_ref[...] = jnp.concatenate(outs, axis=1)


def _selection_matrix(seq):
    nsel, ncp = seq // L_SEL, seq // D_CMP
    stride = L_SEL // D_CMP
    lpad = L_CMP // D_CMP - 1
    n = np.arange(ncp)[None, :]
    j = np.arange(nsel)[:, None]
    m = (n >= stride * j - lpad) & (n <= stride * j + stride - 1)
    return jnp.asarray(m.astype(np.float32), BF16)


def _window_bias():
    r = np.arange(WINDOW + Q_TILE)[:, None]
    c = np.arange(Q_TILE)[None, :]
    offs = [min(i * Q_TILE, WINDOW) for i in range(WINDOW // Q_TILE + 1)]
    masks = [np.where((r <= c + off) & (r > c + off - WINDOW), 0.0, NEG_INF) for off in offs]
    return jnp.asarray(np.stack(masks), F32)


def _nsa_attn(q, qr, gates, bounds, k_cmp, v_cmp, k_sel, v_sel, k_win, v_win, *, batch, seq):
    G, HPG, dh = NSA_GROUPS, HEADS_PER_GROUP, HEAD_DIM
    T = batch * seq
    nt = seq // Q_TILE
    gw = G * HPG * 3
    gt = jnp.transpose(gates[:, :gw].reshape(batch, seq, G, HPG * 3), (0, 2, 3, 1))
    gt = jnp.pad(gt, ((0, 0), (0, 0), (0, 16 - HPG * 3), (0, 0)))
    ncp = seq // D_CMP
    msel = _selection_matrix(seq)
    wbias = _window_bias()
    key_tile = RANK_ROWS * L_SEL
    assert seq % key_tile == 0 and key_tile % Q_TILE == 0
    qspec = pl.BlockSpec((Q_TILE, HPG * dh), lambda b, g, i: (b * nt + i, g))
    cspec = pl.BlockSpec((None, ncp, dh), lambda b, g, i: (b * G + g, 0, 0))
    kspec = pl.BlockSpec((None, None, seq, dh), lambda b, g, i: (b, g, 0, 0))
    return pl.pallas_call(
        _nsa_attn_kernel,
        grid=(batch, G, nt),
        in_specs=[qspec, qspec, pl.BlockSpec((None, None, 16, Q_TILE), lambda b, g, i: (b, g, 0, i)),
                  cspec, cspec, kspec, kspec, kspec, kspec,
                  pl.BlockSpec(msel.shape, lambda b, g, i: (0, 0)),
                  pl.BlockSpec((None,) + wbias.shape[1:],
                               lambda b, g, i: (jnp.minimum(i, wbias.shape[0] - 1), 0, 0)),
                  pl.BlockSpec(memory_space=pltpu.SMEM)],
        out_specs=qspec,
        out_shape=jax.ShapeDtypeStruct((T, G * HPG * dh), F32),
        scratch_shapes=[pltpu.VMEM((seq // key_tile, RANK_ROWS, Q_TILE), F32)],
        compiler_params=_cparams(("parallel", "parallel", "arbitrary")),
        name="nsa_attn",
    )(q, qr, gt, k_cmp, v_cmp, k_sel, v_sel, k_win, v_win, msel, wbias, bounds)


def _logit_bound(q_gain, k_gain):
    rounding = 1.02
    return (HEAD_DIM * HEAD_DIM ** -0.5 * rounding) * jnp.max(jnp.abs(q_gain)) * jnp.max(jnp.abs(k_gain))


def kernel(x, norm_ffn1, ffn1_w_gate, ffn1_w_up, ffn1_w_down, norm_mix, norm_ffn2, ffn2_w_gate, ffn2_w_up, ffn2_w_down, rw_mu, rw_w_rkv, rw_w0, rw_w_lora_a, rw_w_lora_b, rw_a0, rw_a_lora_a, rw_a_lora_b, rw_g_lora_a, rw_g_lora_b, rw_k_k, rw_k_a, rw_r_k, rw_gn_w, rw_gn_b, rw_w_o, kv_norm, kv_w, kv_k_norm, cmp_pos_k, cmp_k_w1, cmp_k_w2, cmp_pos_v, cmp_v_w1, cmp_v_w2, nsa_w_q, nsa_q_norm, nsa_w_o):
    B, S, D = x.shape
    depth = norm_ffn1.shape[0]
    n_rwkv = rw_mu.shape[0]
    assert S % max(Q_TILE, CHUNK) == 0 and S >= WINDOW + Q_TILE and D % 128 == 0
    xt = x.reshape(B * S, D)
    shared = None
    for l in range(depth):
        xt = _ffn(xt, norm_ffn1[l], ffn1_w_gate[l], ffn1_w_up[l], ffn1_w_down[l])
        if l < n_rwkv:
            i = l
            r, lw, k, v, a, g = _rwkv_prep(
                xt, norm_mix[l], rw_mu[i], rw_w_rkv[i], rw_w0[i], rw_w_lora_a[i], rw_w_lora_b[i], rw_a0[i],
                rw_a_lora_a[i], rw_a_lora_b[i], rw_g_lora_a[i], rw_g_lora_b[i], seq=S)
            y = _rwkv_scan(r, lw, k, v, a, rw_k_k[i], rw_k_a[i], rw_r_k[i], rw_gn_w[i], rw_gn_b[i],
                           batch=B, seq=S)
            mixer = (y, g, rw_w_o[i])
        else:
            i = l - n_rwkv
            q, qr, gates = _nsa_q(xt, norm_mix[l], nsa_w_q[i], nsa_q_norm[i], seq=S)
            bounds = jnp.stack([_logit_bound(nsa_q_norm[i], kv_k_norm[br]) for br in range(3)])
            o = _nsa_attn(q, qr, gates, bounds, *shared, batch=B, seq=S)
            mixer = (o, None, nsa_w_o[i])
        xt = _ffn(xt, norm_ffn2[l], ffn2_w_gate[l], ffn2_w_up[l], ffn2_w_down[l], mixer=mixer)
        if l == n_rwkv - 1:
            kc_tok, vc_tok, k_sel, v_sel, k_win, v_win = _nsa_kv(xt, kv_norm, kv_w, kv_k_norm, batch=B, seq=S)
            k_cmp, v_cmp = _nsa_cmp(kc_tok, vc_tok, kv_k_norm[0], cmp_pos_k, cmp_k_w1, cmp_k_w2,
                                    cmp_pos_v, cmp_v_w1, cmp_v_w2, batch=B, seq=S)
            shared = (k_cmp, v_cmp, k_sel, v_sel, k_win, v_win)
    return xt.reshape(B, S, D)
```

```python
import functools

import numpy as np
import jax
import jax.numpy as jnp
from jax import lax
from jax.experimental import pallas as pl
from jax.experimental.pallas import tpu as pltpu

F32 = jnp.float32
BF16 = jnp.bfloat16

NORM_EPS = 1e-6
RW_HEAD_DIM = 64
RW_GN_EPS = 64e-5
NSA_HEADS = 16
NSA_GROUPS = 4
HEADS_PER_GROUP = NSA_HEADS // NSA_GROUPS
HEAD_DIM = 64
L_CMP = 32
D_CMP = 16
L_SEL = 64
N_SEL = 16
WINDOW = 512
ROPE_THETA = 10000.0
FORCE_SCORE = 1e6
NEG_INF = -1e30

CHUNK = 64
INV_BLOCK = 16
Q_TILE = 256
VMEM_LIMIT = 56 * 1024 * 1024


def _cparams(sem):
    return pltpu.CompilerParams(dimension_semantics=sem, vmem_limit_bytes=VMEM_LIMIT)


def _dot(a, b):
    return jnp.dot(a, b, preferred_element_type=F32)


def _dot_nt(a, b):
    return lax.dot_general(a, b, (((1,), (1,)), ((), ())), preferred_element_type=F32)


def _dot_tn(a, b):
    return lax.dot_general(a, b, (((0,), (0,)), ((), ())), preferred_element_type=F32)


def _split3(x):
    hi = x.astype(BF16)
    r1 = x - hi.astype(F32)
    mid = r1.astype(BF16)
    lo = (r1 - mid.astype(F32)).astype(BF16)
    return hi, mid, lo


def _sel_left(m01, x):
    x1, x2, x3 = _split3(x)
    return _dot(m01, x1) + (_dot(m01, x2) + _dot(m01, x3))


def _rms(x, g):
    return x * lax.rsqrt(jnp.mean(x * x, axis=-1, keepdims=True) + NORM_EPS) * g


def _sigmoid(z):
    return 1.0 / (1.0 + jnp.exp(-z))


def _rope(x, cos, sin_signed):
    n = x.shape[-1]
    half = HEAD_DIM // 2
    lane = lax.broadcasted_iota(jnp.int32, x.shape, 1)
    first = (lane & (HEAD_DIM - 1)) < half
    partner = jnp.where(first, pltpu.roll(x, n - half, axis=1), pltpu.roll(x, half, axis=1))
    return x * cos + partner * sin_signed


def _ffn_kernel(*refs, mixer, gated):
    refs = list(refs)
    x = refs.pop(0)[...]
    if mixer:
        y = refs.pop(0)[...]
        if gated:
            y = y * refs.pop(0)[...]
        x = x + _dot(y.astype(BF16), refs.pop(0)[...])
    g_ref, wg_ref, wu_ref, wd_ref, o_ref = refs
    h = _rms(x, g_ref[...]).astype(BF16)
    gate = _dot(h, wg_ref[...])
    up = _dot(h, wu_ref[...])
    act = (gate * _sigmoid(gate) * up).astype(BF16)
    o_ref[...] = x + 0.5 * _dot(act, wd_ref[...])


def _resident(shape):
    return pl.BlockSpec(shape, lambda *_: (0,) * len(shape), pipeline_mode=pl.Buffered(1))


def _ffn(x, g, wg, wu, wd, *, mixer=None, tm=512):
    T, D = x.shape
    tm = min(tm, T)
    tile = pl.BlockSpec((tm, D), lambda i: (i, 0))
    ins, specs = [x], [tile]
    if mixer is not None:
        y, y_gate, w_o = mixer
        ins += [y] + ([y_gate] if y_gate is not None else []) + [w_o.astype(BF16)]
        specs += [tile] * (len(ins) - 2) + [_resident(w_o.shape)]
    ins += [g.reshape(1, D), wg.astype(BF16), wu.astype(BF16), wd.astype(BF16)]
    specs += [_resident((1, D)), _resident(wg.shape), _resident(wu.shape), _resident(wd.shape)]
    return pl.pallas_call(
        functools.partial(_ffn_kernel, mixer=mixer is not None, gated=mixer is not None and mixer[1] is not None),
        grid=(T // tm,),
        in_specs=specs,
        out_specs=tile,
        out_shape=jax.ShapeDtypeStruct((T, D), F32),
        compiler_params=_cparams(("parallel",)),
        name="ffn",
    )(*ins)


def _rwkv_prep_kernel(x_ref, xp_ref, gm_ref, mu_ref, wrkv_ref, w0_ref, wla_ref, wlb_ref, a0_ref,
                      ala_ref, alb_ref, gla_ref, glb_ref,
                      r_ref, lw_ref, k_ref, v_ref, a_ref, g_ref, *, tiles_per_seq):
    i = pl.program_id(0)
    gm = gm_ref[...]
    h = _rms(x_ref[...], gm)
    hp = _rms(xp_ref[...], gm)[7:8, :]
    hp = jnp.where(i % tiles_per_seq == 0, 0.0, hp)
    row = lax.broadcasted_iota(jnp.int32, h.shape, 0)
    prev = jnp.where(row == 0, hp, pltpu.roll(h, 1, axis=0))
    xx = prev - h
    mu = mu_ref[...]
    mix = lambda c: (h + xx * mu[c:c + 1, :]).astype(BF16)
    r_ref[...] = _dot(mix(0), wrkv_ref[0])
    k_ref[...] = _dot(mix(2), wrkv_ref[1])
    v_ref[...] = _dot(mix(3), wrkv_ref[2])
    z = w0_ref[...] + _dot(jnp.tanh(_dot(mix(1), wla_ref[...])).astype(BF16), wlb_ref[...])
    softplus = jnp.maximum(-z, 0.0) + jnp.log(1.0 + jnp.exp(-jnp.abs(z)))
    lw_ref[...] = -jnp.exp(-softplus - 0.5)
    a_ref[...] = _sigmoid(a0_ref[...] + _dot(_dot(mix(4), ala_ref[...]).astype(BF16), alb_ref[...]))
    g_ref[...] = _dot(_sigmoid(_dot(mix(5), gla_ref[...])).astype(BF16), glb_ref[...])


def _rwkv_prep(x, gmix, mu, w_rkv, w0, wla, wlb, a0, ala, alb, gla, glb, *, seq, tm=512):
    T, D = x.shape
    tm = min(tm, seq)
    full = lambda a: _resident(a.shape)
    row = lambda a: a.reshape(1, D)
    args = [gmix.reshape(1, D), mu, w_rkv.astype(BF16), row(w0), wla.astype(BF16), wlb.astype(BF16),
            row(a0), ala.astype(BF16), alb.astype(BF16), gla.astype(BF16), glb.astype(BF16)]
    out = jax.ShapeDtypeStruct((T, D), F32)
    return pl.pallas_call(
        functools.partial(_rwkv_prep_kernel, tiles_per_seq=seq // tm),
        grid=(T // tm,),
        in_specs=[pl.BlockSpec((tm, D), lambda i: (i, 0)),
                  pl.BlockSpec((8, D), lambda i: (jnp.maximum(i * (tm // 8) - 1, 0), 0))]
                 + [full(a) for a in args],
        out_specs=[pl.BlockSpec((tm, D), lambda i: (i, 0))] * 6,
        out_shape=[out] * 6,
        compiler_params=_cparams(("parallel",)),
        name="rwkv_prep",
    )(x, x, *args)


PAIR = 2 * RW_HEAD_DIM


def _blockdiag(x, masks):
    return jnp.concatenate([x * masks[0], x * masks[1]], axis=0)


def _pair_dot(lhs, rhs_list, masks, transpose_rhs=False):
    dot = _dot_nt if transpose_rhs else _dot
    rhs = jnp.concatenate([_blockdiag(r.astype(BF16), masks) for r in rhs_list], axis=0 if transpose_rhs else 1)
    out = dot(lhs.astype(BF16), rhs)
    return [out[:, i * PAIR:(i + 1) * PAIR] for i in range(len(rhs_list))]


def _seg_sum(x, is_a):
    zero = jnp.zeros_like(x)
    sa = jnp.sum(jnp.where(is_a, x, zero), axis=-1, keepdims=True)
    sb = jnp.sum(jnp.where(is_a, zero, x), axis=-1, keepdims=True)
    return jnp.where(is_a, sa, sb)


def _rwkv_chunk(r, lw, k, v, a, kk_p, ka_p, rk_p, gn_w, gn_b, state, consts):
    is_a, masks, tri_incl, tri_strict, blockdiag, eye, ltri = consts
    inv_n = 1.0 / RW_HEAD_DIM
    kkr = k * kk_p
    kk = kkr / jnp.maximum(jnp.sqrt(_seg_sum(kkr * kkr, is_a)), 1e-12)
    k2 = k * (1.0 + (a - 1.0) * ka_p)
    kb = kk * a

    cum = _sel_left(ltri, lw)
    yield
    cum_last = cum[CHUNK - 1:CHUNK, :]
    einv = jnp.exp(-cum)
    elast = jnp.exp(cum_last - cum)
    a_t = -kk * jnp.exp(cum - lw)
    r_t = r * jnp.exp(cum)
    b_h = kb * einv
    k_h = k2 * einv
    b_l = kb * elast
    k_l = k2 * elast

    ar = jnp.concatenate([a_t, r_t], axis=0)
    d1 = lambda p, qs: _pair_dot(p, qs, masks)
    sc_b, sc_k = _pair_dot(ar, [b_h, k_h], masks, transpose_rhs=True)
    yield
    m_ab = jnp.where(tri_strict, sc_b[:CHUNK], 0.0)
    m_rb = jnp.where(tri_incl, sc_b[CHUNK:], 0.0)
    m_ak = jnp.where(tri_strict, sc_k[:CHUNK], 0.0)
    m_rk = jnp.where(tri_incl, sc_k[CHUNK:], 0.0)

    (w1,) = d1(m_ak, [v])
    m_d = jnp.where(blockdiag, m_ab, 0.0)
    m_o = m_ab - m_d
    dinv = eye + m_d
    pw = m_d
    for _ in range(int(np.log2(INV_BLOCK)) - 1):
        (pw,) = d1(pw, [pw])
        yield
        dinv = dinv + d1(dinv, [pw])[0]
        yield
    n1, x1, x2 = d1(dinv, [m_o, a_t, w1])
    yield
    (n2,) = d1(n1, [n1])
    yield
    u1, u2 = d1(n2, [x1, x2])
    x1, x2 = x1 + u1, x2 + u2
    yield
    u1, u2 = d1(n1, [x1, x2])
    x1, x2 = x1 + u1, x2 + u2
    yield

    (z_g,) = d1(m_rb, [x1])
    g1 = r_t + z_g
    x2b, vb = x2.astype(BF16), v.astype(BF16)
    y_c = _dot(jnp.concatenate([m_rb, m_rk], axis=1).astype(BF16),
               jnp.concatenate([_blockdiag(x2b, masks), _blockdiag(vb, masks)], axis=0))
    z2 = _dot_tn(jnp.concatenate([b_l, k_l], axis=0).astype(BF16),
                 jnp.concatenate([jnp.concatenate([x1.astype(BF16), x2b], axis=1),
                                  jnp.concatenate([jnp.zeros_like(vb), vb], axis=1)], axis=0))
    yield
    diag_blocks = lambda f: jnp.where(is_a, f[:RW_HEAD_DIM], f[RW_HEAD_DIM:])
    trans = jnp.where(eye > 0, jnp.exp(cum_last), 0.0) + diag_blocks(z2[:, :PAIR])
    add = diag_blocks(z2[:, PAIR:])
    (ys,) = d1(jnp.concatenate([g1, trans], axis=0), [state])
    yield
    y = ys[:CHUNK] + y_c
    new_state = ys[CHUNK:] + add

    mean = _seg_sum(y, is_a) * inv_n
    dev = y - mean
    var = _seg_sum(dev * dev, is_a) * inv_n
    yn = dev * lax.rsqrt(var + RW_GN_EPS) * gn_w + gn_b
    bonus = _seg_sum(r * k2 * rk_p, is_a) * v
    return yn + bonus, new_state


def _rwkv_scan_kernel(r_ref, lw_ref, k_ref, v_ref, a_ref, kk_ref, ka_ref, rk_ref, gw_ref, gb_ref, o_ref,
                      state_ref, *, n_chunks, pairs):
    @pl.when(pl.program_id(2) == 0)
    def _():
        state_ref[...] = jnp.zeros_like(state_ref)

    row = lax.broadcasted_iota(jnp.int32, (CHUNK, PAIR), 0)
    lane = lax.broadcasted_iota(jnp.int32, (CHUNK, PAIR), 1)
    col = lane & (RW_HEAD_DIM - 1)
    is_a = lane < RW_HEAD_DIM
    blockdiag = (row // INV_BLOCK) == (col // INV_BLOCK)
    eye = jnp.where(row == col, 1.0, 0.0).astype(F32)
    row_s = lax.broadcasted_iota(jnp.int32, (CHUNK, CHUNK), 0)
    col_s = lax.broadcasted_iota(jnp.int32, (CHUNK, CHUNK), 1)
    ltri = jnp.where(row_s >= col_s, 1.0, 0.0).astype(BF16)
    masks = (jnp.where(is_a, 1.0, 0.0).astype(BF16), jnp.where(is_a, 0.0, 1.0).astype(BF16))
    consts = (is_a, masks, row >= col, row > col, blockdiag, eye, ltri)

    chains = [(s, p) for s in range(r_ref.shape[0]) for p in range(pairs)]

    def body(c, carry):
        t0 = pl.multiple_of(c * CHUNK, CHUNK)
        gens = []
        for s, p in chains:
            sl = slice(p * PAIR, (p + 1) * PAIR)
            ld = lambda ref: ref[s, pl.ds(t0, CHUNK), sl]
            gens.append(_rwkv_chunk(ld(r_ref), ld(lw_ref), ld(k_ref), ld(v_ref), ld(a_ref),
                                    kk_ref[:, sl], ka_ref[:, sl], rk_ref[:, sl], gw_ref[:, sl], gb_ref[:, sl],
                                    state_ref[s * pairs + p], consts))
        results = [None] * len(chains)
        while any(res is None for res in results):
            for i, gen in enumerate(gens):
                if results[i] is None:
                    try:
                        next(gen)
                    except StopIteration as done:
                        results[i] = done.value
        for (s, p), (y, st) in zip(chains, results):
            state_ref[s * pairs + p] = st
            o_ref[s, pl.ds(t0, CHUNK), p * PAIR:(p + 1) * PAIR] = y
        return carry

    lax.fori_loop(0, n_chunks, body, 0)


def _rwkv_scan(r, lw, k, v, a, k_k, k_a, r_k, gn_w, gn_b, *, batch, seq, pairs=8, seqs=2, tt=256):
    T, D = r.shape
    tt = min(tt, seq)
    seqs = min(seqs, batch)
    lanes = pairs * PAIR
    seq_spec = pl.BlockSpec((seqs, tt, lanes), lambda b, p, t: (b, t, p))
    par_spec = pl.BlockSpec((1, lanes), lambda b, p, t: (0, p))
    as3 = lambda t: t.reshape(batch, seq, D)
    row = lambda t: t.reshape(1, D)
    y = pl.pallas_call(
        functools.partial(_rwkv_scan_kernel, n_chunks=tt // CHUNK, pairs=pairs),
        grid=(batch // seqs, D // lanes, seq // tt),
        in_specs=[seq_spec] * 5 + [par_spec] * 5,
        out_specs=seq_spec,
        out_shape=jax.ShapeDtypeStruct((batch, seq, D), F32),
        scratch_shapes=[pltpu.VMEM((seqs * pairs, RW_HEAD_DIM, PAIR), F32)],
        compiler_params=_cparams(("parallel", "parallel", "arbitrary")),
        name="rwkv_scan",
    )(as3(r), as3(lw), as3(k), as3(v), as3(a), row(k_k), row(k_a), row(r_k), row(gn_w), row(gn_b))
    return y.reshape(T, D)


def _group_rms(x, gain, pool, expand):
    pooled = _dot((x * x).astype(BF16), pool)
    inv = lax.rsqrt(pooled * (1.0 / HEAD_DIM) + NORM_EPS)
    hi = inv.astype(BF16)
    lo = (inv - hi.astype(F32)).astype(BF16)
    return x * (_dot(hi, expand) + _dot(lo, expand)) * gain


def _nsa_kv_kernel(x_ref, gn_ref, w_ref, kn_ref, cos_ref, sin_ref, pool_ref, exp_ref,
                   kc_ref, vc_ref, ks_ref, vs_ref, kw_ref, vw_ref):
    h = _rms(x_ref[...], gn_ref[...]).astype(BF16)
    kv = _dot(h, w_ref[...])
    W = NSA_GROUPS * HEAD_DIM
    part = lambda i: kv[:, i * W:(i + 1) * W]
    cos, sin = cos_ref[...], sin_ref[...]
    pool, expand = pool_ref[...], exp_ref[...]

    def put(ref, val):
        for g in range(NSA_GROUPS):
            ref[g] = val[:, g * HEAD_DIM:(g + 1) * HEAD_DIM].astype(ref.dtype)

    put(kc_ref, part(0))
    put(vc_ref, part(1))
    put(ks_ref, _rope(_group_rms(part(2), kn_ref[1:2, :], pool, expand), cos, sin))
    put(vs_ref, part(3))
    put(kw_ref, _rope(_group_rms(part(4), kn_ref[2:3, :], pool, expand), cos, sin))
    put(vw_ref, part(5))


def _pool_matrices(width):
    head = np.arange(width) // HEAD_DIM
    pool = (head[:, None] == np.arange(128)[None, :]).astype(np.float32)
    return jnp.asarray(pool, BF16), jnp.asarray(pool.T, BF16)


def _rope_tables(seq, width):
    half = HEAD_DIM // 2
    inv = ROPE_THETA ** (-jnp.arange(half, dtype=F32) / half)
    ang = jnp.arange(seq).astype(F32)[:, None] * inv[None, :]
    cos, sin = jnp.cos(ang), jnp.sin(ang)
    reps = width // HEAD_DIM
    return (jnp.tile(jnp.concatenate([cos, cos], axis=1), (1, reps)),
            jnp.tile(jnp.concatenate([-sin, sin], axis=1), (1, reps)))


def _nsa_kv(x, kv_norm, kv_w, kv_k_norm, *, batch, seq, tm=512):
    T, D = x.shape
    tm = min(tm, seq)
    G, dh = NSA_GROUPS, HEAD_DIM
    W = G * dh
    cos, sin = _rope_tables(seq, W)
    pool, expand = _pool_matrices(W)
    kn = jnp.tile(kv_k_norm, (1, G))
    tps = seq // tm
    full = lambda a: pl.BlockSpec(a.shape, lambda b, t: (0,) * a.ndim)
    tab = pl.BlockSpec((tm, W), lambda b, t: (t, 0))
    out_tile = pl.BlockSpec((None, G, tm, dh), lambda b, t: (b, 0, t, 0))
    w = kv_w.astype(BF16)
    shape = (batch, G, seq, dh)
    return pl.pallas_call(
        _nsa_kv_kernel,
        grid=(batch, tps),
        in_specs=[pl.BlockSpec((tm, D), lambda b, t: (b * tps + t, 0)), pl.BlockSpec((1, D), lambda b, t: (0, 0)),
                  full(w), full(kn), tab, tab, full(pool), full(expand)],
        out_specs=[out_tile] * 6,
        out_shape=[jax.ShapeDtypeStruct(shape, F32)] * 2 + [jax.ShapeDtypeStruct(shape, BF16)] * 4,
        compiler_params=_cparams(("parallel", "parallel")),
        name="nsa_kv",
    )(x, kv_norm.reshape(1, D), w, kn, cos, sin, pool, expand)


def _compress(tok_ref, pos_ref, w1_ref, w2_ref):
    nch = tok_ref.shape[0] // D_CMP
    dh = tok_ref.shape[1]
    lo = hi = None
    for l in range(D_CMP):
        rows = tok_ref[pl.ds(l, nch, stride=D_CMP), :]
        for half in range(L_CMP // D_CMP):
            p = half * D_CMP + l
            term = _dot((rows + pos_ref[p:p + 1, :]).astype(BF16), w1_ref[p * dh:(p + 1) * dh, :])
            if half == 0:
                lo = term if lo is None else lo + term
            else:
                hi = term if hi is None else hi + term
    hid = lo + pltpu.roll(hi, nch - 1, axis=0)
    act = 0.5 * hid * (1.0 + jnp.tanh(np.sqrt(2.0 / np.pi) * (hid + 0.044715 * hid * hid * hid)))
    return _dot(act.astype(BF16), w2_ref[...])


def _nsa_cmp_kernel(ck_ref, cv_ref, pk_ref, pv_ref, w1k_ref, w2k_ref, w1v_ref, w2v_ref, kn_ref,
                    ko_ref, vo_ref):
    ko_ref[...] = _rms(_compress(ck_ref, pk_ref, w1k_ref, w2k_ref), kn_ref[...])
    vo_ref[...] = _compress(cv_ref, pv_ref, w1v_ref, w2v_ref)


def _nsa_cmp(kc_tok, vc_tok, kn0, pos_k, w1k, w2k, pos_v, w1v, w2v, *, batch, seq):
    G, dh = NSA_GROUPS, HEAD_DIM
    assert L_CMP == 2 * D_CMP
    nch = seq // D_CMP
    full = lambda a: pl.BlockSpec(a.shape, lambda b, g: (0,) * a.ndim)
    blk = pl.BlockSpec((None, None, seq, dh), lambda b, g: (b, g, 0, 0))
    oblk = pl.BlockSpec((None, nch, dh), lambda b, g: (b * G + g, 0, 0))
    args = [pos_k, pos_v, w1k.astype(BF16), w2k.astype(BF16), w1v.astype(BF16), w2v.astype(BF16),
            kn0.reshape(1, dh)]
    return pl.pallas_call(
        _nsa_cmp_kernel,
        grid=(batch, G),
        in_specs=[blk, blk] + [full(a) for a in args],
        out_specs=[oblk, oblk],
        out_shape=[jax.ShapeDtypeStruct((batch * G, nch, dh), F32)] * 2,
        compiler_params=_cparams(("parallel", "parallel")),
        name="nsa_cmp",
    )(kc_tok, vc_tok, *args)


def _nsa_q_kernel(x_ref, gn_ref, wq_ref, wg_ref, qn_ref, cos_ref, sin_ref, pool_ref, exp_ref,
                  q_ref, qr_ref, gate_ref):
    h = _rms(x_ref[...], gn_ref[...]).astype(BF16)
    q = _group_rms(_dot(h, wq_ref[...]), qn_ref[...], pool_ref[...], exp_ref[...])
    reps = q.shape[1] // cos_ref.shape[1]
    cos = jnp.concatenate([cos_ref[...]] * reps, axis=1)
    sin = jnp.concatenate([sin_ref[...]] * reps, axis=1)
    scale = HEAD_DIM ** -0.5
    q_ref[...] = (q * scale).astype(BF16)
    qr_ref[...] = (_rope(q, cos, sin) * scale).astype(BF16)
    gate_ref[...] = _sigmoid(_dot(h, wg_ref[...]))


def _nsa_q(x, gmix, w_q, q_norm, *, seq, tm=512):
    T, D = x.shape
    tm = min(tm, seq)
    HD = NSA_HEADS * HEAD_DIM
    cos, sin = _rope_tables(seq, 128)
    pool, expand = _pool_matrices(HD)
    wq = w_q[:, :HD].astype(BF16)
    wg = jnp.pad(w_q[:, HD:], ((0, 0), (0, 128 - 3 * NSA_HEADS))).astype(BF16)
    qn = jnp.tile(q_norm.reshape(1, HEAD_DIM), (1, NSA_HEADS))
    tps = seq // tm
    full = lambda a: pl.BlockSpec(a.shape, lambda i: (0,) * a.ndim)
    tab = pl.BlockSpec((tm, 128), lambda i: (i % tps, 0))
    tile = lambda w: pl.BlockSpec((tm, w), lambda i: (i, 0))
    return pl.pallas_call(
        _nsa_q_kernel,
        grid=(T // tm,),
        in_specs=[tile(D), pl.BlockSpec((1, D), lambda i: (0, 0)), full(wq), full(wg), full(qn),
                  tab, tab, full(pool), full(expand)],
        out_specs=[tile(HD), tile(HD), tile(128)],
        out_shape=[jax.ShapeDtypeStruct((T, HD), BF16)] * 2 + [jax.ShapeDtypeStruct((T, 128), F32)],
        compiler_params=_cparams(("parallel",)),
        name="nsa_q",
    )(x, gmix.reshape(1, D), wq, wg, qn, cos, sin, pool, expand)


RANK_ROWS = 8
MAX_SAFE_SHIFT = 30.0


def _topk_bias(score, n_top):
    nsel = score.shape[0]
    groups = [score[lo:lo + RANK_ROWS, :] for lo in range(0, nsel, RANK_ROWS)]
    ranks = [jnp.zeros(g.shape, jnp.int32) for g in groups]
    for i in range(nsel):
        si = score[i:i + 1, :]
        for gi, sg in enumerate(groups):
            lo = gi * RANK_ROWS
            if lo > i:
                beats = si >= sg
            elif lo + RANK_ROWS - 1 < i:
                beats = si > sg
            else:
                later = lax.broadcasted_iota(jnp.int32, sg.shape, 0) + lo > i
                beats = (si > sg) | ((si == sg) & later)
            ranks[gi] = ranks[gi] + jnp.where(beats, 1, 0)
    return [jnp.where(r < n_top, 0.0, NEG_INF) for r in ranks]


def _nsa_attn_kernel(*refs):
    bound_ref = refs[11]
    safe = ((bound_ref[0] <= MAX_SAFE_SHIFT) & (bound_ref[1] <= MAX_SAFE_SHIFT)
            & (bound_ref[2] <= MAX_SAFE_SHIFT))
    pl.when(safe)(functools.partial(_nsa_attn_step, *refs, use_bound=True))
    pl.when(jnp.logical_not(safe))(functools.partial(_nsa_attn_step, *refs, use_bound=False))


def _nsa_attn_step(q_ref, qr_ref, gate_ref, kc_ref, vc_ref, ks_ref, vs_ref, kw_ref, vw_ref, msel_ref,
                   wbias_ref, bound_ref, o_ref, bias_ref, *, use_bound):
    TQ, HPG, dh = Q_TILE, HEADS_PER_GROUP, HEAD_DIM
    NQ = HPG * TQ
    qi = pl.program_id(2)
    s0 = qi * TQ
    heads = lambda ref: [ref[:, hh * dh:(hh + 1) * dh] for hh in range(HPG)]
    q_heads, qr_heads = heads(q_ref), heads(qr_ref)
    qk = lambda keys, qs: jnp.concatenate([_dot_nt(keys, qh) for qh in qs], axis=1)
    over_heads = lambda a: jnp.concatenate([a] * HPG, axis=1)
    rows = lambda shape: lax.broadcasted_iota(jnp.int32, shape, 0)
    cols = lambda shape: lax.broadcasted_iota(jnp.int32, shape, 1)

    def softmax_then(masked_logits, bound, finish):
        if use_bound:
            e = jnp.exp(masked_logits(bound))
        else:
            s = masked_logits(0.0)
            e = jnp.exp(s - jnp.max(s, axis=0, keepdims=True))
        return finish(e, jnp.sum(e, axis=0, keepdims=True))

    ncp = kc_ref.shape[0]
    shape = (ncp, TQ)
    cmp_bias = jnp.where(rows(shape) * D_CMP + (L_CMP - 1) <= s0 + cols(shape), 0.0, NEG_INF)
    sees_any = s0 + (cols((1, NQ)) & (TQ - 1)) >= L_CMP - 1

    def cmp_finish(e, l):
        p = e * jnp.where(sees_any, 1.0 / l, 0.0)
        imp = p[:, 0:TQ]
        for hh in range(1, HPG):
            imp = imp + p[:, hh * TQ:(hh + 1) * TQ]
        return _dot_tn(vc_ref[...].astype(BF16), p.astype(BF16)), imp

    o_cmp, imp = softmax_then(lambda shift: qk(kc_ref[...].astype(BF16), q_heads) + over_heads(cmp_bias - shift),
                              bound_ref[0], cmp_finish)

    p_slc = _sel_left(msel_ref[...], imp)
    nsel = p_slc.shape[0]
    shape = (nsel, TQ)
    blk = rows(shape)
    t = s0 + cols(shape)
    cur = t >> int(np.log2(L_SEL))
    forced = (blk == 0) | (blk == cur) | (blk == cur - 1)
    score = jnp.where(forced, FORCE_SCORE, p_slc)
    score = jnp.where(blk * L_SEL > t, -1.0, score)
    for gi, group_bias in enumerate(_topk_bias(score, min(N_SEL, nsel))):
        bias_ref[gi] = group_bias

    KT = RANK_ROWS * L_SEL
    last = (qi * TQ) // KT
    shape = (KT, TQ)
    causal_bias = jnp.where(rows(shape) <= cols(shape) + (s0 - last * KT), 0.0, NEG_INF)

    def tile_logits(j):
        return qk(ks_ref[pl.ds(pl.multiple_of(j * KT, KT), KT), :], qr_heads)

    def sel_slabs(logits, j, diagonal, shift):
        bias = bias_ref[j]
        for b in range(RANK_ROWS):
            blk_rows = slice(b * L_SEL, (b + 1) * L_SEL)
            mask = bias[b:b + 1, :] + causal_bias[blk_rows, :] if diagonal else bias[b:b + 1, :]
            yield logits[blk_rows, :] + (over_heads(mask) - shift)

    def sel_max(j, m, diagonal):
        for s in sel_slabs(tile_logits(j), j, diagonal, 0.0):
            m = jnp.maximum(m, jnp.max(s, axis=0, keepdims=True))
        return m

    def true_max():
        m = lax.fori_loop(0, last, lambda j, m: sel_max(j, m, False), jnp.full((1, NQ), NEG_INF, F32))
        return sel_max(last, m, True)

    m_sel = jnp.full((1, NQ), bound_ref[1], F32) if use_bound else true_max()

    def sel_accumulate(tiles, carry, diagonal):
        l, acc = carry
        for j, logits in [(j, tile_logits(j)) for j in tiles]:
            ps = []
            for s in sel_slabs(logits, j, diagonal, m_sel):
                p = jnp.exp(s)
                l = l + jnp.sum(p.reshape(L_SEL // 8, 8, NQ), axis=0)
                ps.append(p.astype(BF16))
            acc = acc + _dot_tn(vs_ref[pl.ds(pl.multiple_of(j * KT, KT), KT), :], jnp.concatenate(ps, axis=0))
        return l, acc

    carry = (jnp.zeros((8, NQ), F32), jnp.zeros((dh, NQ), F32))
    carry = lax.fori_loop(0, last // 2, lambda i, c: sel_accumulate([2 * i, 2 * i + 1], c, False), carry)
    carry = lax.fori_loop(2 * (last // 2), last, lambda j, c: sel_accumulate([j], c, False), carry)
    l_sel, acc_sel = sel_accumulate([last], carry, True)
    o_sel = acc_sel / jnp.sum(l_sel, axis=0, keepdims=True)

    span = WINDOW + TQ
    start = pl.multiple_of(jnp.maximum(s0 - WINDOW, 0), TQ)
    o_win = softmax_then(
        lambda shift: qk(kw_ref[pl.ds(start, span), :], qr_heads) + over_heads(wbias_ref[...] - shift),
        bound_ref[2], lambda e, l: _dot_tn(vw_ref[pl.ds(start, span), :], e.astype(BF16)) / l)

    outs = []
    for hh in range(HPG):
        sl = slice(hh * TQ, (hh + 1) * TQ)
        g = lambda c: gate_ref[3 * hh + c:3 * hh + c + 1, :]
        o_h = g(0) * o_cmp[:, sl] + g(1) * o_sel[:, sl] + g(2) * o_win[:, sl]
        outs.append(o_h.T)
    o_ref[...] = jnp.concatenate(outs, axis=1)


def _selection_matrix(seq):
    nsel, ncp = seq // L_SEL, seq // D_CMP
    stride = L_SEL // D_CMP
    lpad = L_CMP // D_CMP - 1
    n = np.arange(ncp)[None, :]
    j = np.arange(nsel)[:, None]
    m = (n >= stride * j - lpad) & (n <= stride * j + stride - 1)
    return jnp.asarray(m.astype(np.float32), BF16)


def _window_bias():
    r = np.arange(WINDOW + Q_TILE)[:, None]
    c = np.arange(Q_TILE)[None, :]
    offs = [min(i * Q_TILE, WINDOW) for i in range(WINDOW // Q_TILE + 1)]
    masks = [np.where((r <= c + off) & (r > c + off - WINDOW), 0.0, NEG_INF) for off in offs]
    return jnp.asarray(np.stack(masks), F32)


def _nsa_attn(q, qr, gates, bounds, k_cmp, v_cmp, k_sel, v_sel, k_win, v_win, *, batch, seq):
    G, HPG, dh = NSA_GROUPS, HEADS_PER_GROUP, HEAD_DIM
    T = batch * seq
    nt = seq // Q_TILE
    gw = G * HPG * 3
    gt = jnp.transpose(gates[:, :gw].reshape(batch, seq, G, HPG * 3), (0, 2, 3, 1))
    gt = jnp.pad(gt, ((0, 0), (0, 0), (0, 16 - HPG * 3), (0, 0)))
    ncp = seq // D_CMP
    msel = _selection_matrix(seq)
    wbias = _window_bias()
    key_tile = RANK_ROWS * L_SEL
    assert seq % key_tile == 0 and key_tile % Q_TILE == 0
    qspec = pl.BlockSpec((Q_TILE, HPG * dh), lambda b, g, i: (b * nt + i, g))
    cspec = pl.BlockSpec((None, ncp, dh), lambda b, g, i: (b * G + g, 0, 0))
    kspec = pl.BlockSpec((None, None, seq, dh), lambda b, g, i: (b, g, 0, 0))
    return pl.pallas_call(
        _nsa_attn_kernel,
        grid=(batch, G, nt),
        in_specs=[qspec, qspec, pl.BlockSpec((None, None, 16, Q_TILE), lambda b, g, i: (b, g, 0, i)),
                  cspec, cspec, kspec, kspec, kspec, kspec,
                  pl.BlockSpec(msel.shape, lambda b, g, i: (0, 0)),
                  pl.BlockSpec((None,) + wbias.shape[1:],
                               lambda b, g, i: (jnp.minimum(i, wbias.shape[0] - 1), 0, 0)),
                  pl.BlockSpec(memory_space=pltpu.SMEM)],
        out_specs=qspec,
        out_shape=jax.ShapeDtypeStruct((T, G * HPG * dh), F32),
        scratch_shapes=[pltpu.VMEM((seq // key_tile, RANK_ROWS, Q_TILE), F32)],
        compiler_params=_cparams(("parallel", "parallel", "arbitrary")),
        name="nsa_attn",
    )(q, qr, gt, k_cmp, v_cmp, k_sel, v_sel, k_win, v_win, msel, wbias, bounds)


def _logit_bound(q_gain, k_gain):
    rounding = 1.02
    return (HEAD_DIM * HEAD_DIM ** -0.5 * rounding) * jnp.max(jnp.abs(q_gain)) * jnp.max(jnp.abs(k_gain))


def kernel(x, norm_ffn1, ffn1_w_gate, ffn1_w_up, ffn1_w_down, norm_mix, norm_ffn2, ffn2_w_gate, ffn2_w_up, ffn2_w_down, rw_mu, rw_w_rkv, rw_w0, rw_w_lora_a, rw_w_lora_b, rw_a0, rw_a_lora_a, rw_a_lora_b, rw_g_lora_a, rw_g_lora_b, rw_k_k, rw_k_a, rw_r_k, rw_gn_w, rw_gn_b, rw_w_o, kv_norm, kv_w, kv_k_norm, cmp_pos_k, cmp_k_w1, cmp_k_w2, cmp_pos_v, cmp_v_w1, cmp_v_w2, nsa_w_q, nsa_q_norm, nsa_w_o):
    B, S, D = x.shape
    depth = norm_ffn1.shape[0]
    n_rwkv = rw_mu.shape[0]
    assert S % max(Q_TILE, CHUNK) == 0 and S >= WINDOW + Q_TILE and D % 128 == 0
    xt = x.reshape(B * S, D)
    shared = None
    for l in range(depth):
        xt = _ffn(xt, norm_ffn1[l], ffn1_w_gate[l], ffn1_w_up[l], ffn1_w_down[l])
        if l < n_rwkv:
            i = l
            r, lw, k, v, a, g = _rwkv_prep(
                xt, norm_mix[l], rw_mu[i], rw_w_rkv[i], rw_w0[i], rw_w_lora_a[i], rw_w_lora_b[i], rw_a0[i],
                rw_a_lora_a[i], rw_a_lora_b[i], rw_g_lora_a[i], rw_g_lora_b[i], seq=S)
            y = _rwkv_scan(r, lw, k, v, a, rw_k_k[i], rw_k_a[i], rw_r_k[i], rw_gn_w[i], rw_gn_b[i],
                           batch=B, seq=S)
            mixer = (y, g, rw_w_o[i])
        else:
            i = l - n_rwkv
            q, qr, gates = _nsa_q(xt, norm_mix[l], nsa_w_q[i], nsa_q_norm[i], seq=S)
            bounds = jnp.stack([_logit_bound(nsa_q_norm[i], kv_k_norm[br]) for br in range(3)])
            o = _nsa_attn(q, qr, gates, bounds, *shared, batch=B, seq=S)
            mixer = (o, None, nsa_w_o[i])
        xt = _ffn(xt, norm_ffn2[l], ffn2_w_gate[l], ffn2_w_up[l], ffn2_w_down[l], mixer=mixer)
        if l == n_rwkv - 1:
            kc_tok, vc_tok, k_sel, v_sel, k_win, v_win = _nsa_kv(xt, kv_norm, kv_w, kv_k_norm, batch=B, seq=S)
            k_cmp, v_cmp = _nsa_cmp(kc_tok, vc_tok, kv_k_norm[0], cmp_pos_k, cmp_k_w1, cmp_k_w2,
                                    cmp_pos_v, cmp_v_w1, cmp_v_w2, batch=B, seq=S)
            shared = (k_cmp, v_cmp, k_sel, v_sel, k_win, v_win)
    return xt.reshape(B, S, D)
```

```python
import functools

import numpy as np
import jax
import jax.numpy as jnp
from jax import lax
from jax.experimental import pallas as pl
from jax.experimental.pallas import tpu as pltpu

F32 = jnp.float32
BF16 = jnp.bfloat16

NORM_EPS = 1e-6
RW_HEAD_DIM = 64
RW_GN_EPS = 64e-5
NSA_HEADS = 16
NSA_GROUPS = 4
HEADS_PER_GROUP = NSA_HEADS // NSA_GROUPS
HEAD_DIM = 64
L_CMP = 32
D_CMP = 16
L_SEL = 64
N_SEL = 16
WINDOW = 512
ROPE_THETA = 10000.0
FORCE_SCORE = 1e6
NEG_INF = -1e30
LOGIT_SCALE = HEAD_DIM ** -0.5 * float(np.log2(np.e))

LANES = 128
SUBLANES = 8
PACKED_SUBLANES = 16
VMEM_LIMIT = 56 * 1024 * 1024

CHUNK = 64
INV_BLOCK = 16
Q_TILE = 256
GATE_ROWS = PACKED_SUBLANES


def _cparams(sem):
    return pltpu.CompilerParams(dimension_semantics=sem, vmem_limit_bytes=VMEM_LIMIT)


def _dot(a, b):
    return jnp.dot(a, b, preferred_element_type=F32)


def _dot_nt(a, b):
    return lax.dot_general(a, b, (((1,), (1,)), ((), ())), preferred_element_type=F32)


def _dot_tn(a, b):
    return lax.dot_general(a, b, (((0,), (0,)), ((), ())), preferred_element_type=F32)


def _split3(x):
    hi = x.astype(BF16)
    r1 = x - hi.astype(F32)
    mid = r1.astype(BF16)
    lo = (r1 - mid.astype(F32)).astype(BF16)
    return hi, mid, lo


def _sel_left(m01, x):
    x1, x2, x3 = _split3(x)
    return _dot(m01, x1) + (_dot(m01, x2) + _dot(m01, x3))


def _rms(x, g):
    return x * lax.rsqrt(jnp.mean(x * x, axis=-1, keepdims=True) + NORM_EPS) * g


def _sigmoid(z):
    return 1.0 / (1.0 + jnp.exp(-z))


def _rope(x, cos, sin_signed):
    n = x.shape[-1]
    half = HEAD_DIM // 2
    lane = lax.broadcasted_iota(jnp.int32, x.shape, 1)
    first = (lane & (HEAD_DIM - 1)) < half
    partner = jnp.where(first, pltpu.roll(x, n - half, axis=1), pltpu.roll(x, half, axis=1))
    return x * cos + partner * sin_signed


def _ffn_kernel(*refs, mixer, gated):
    refs = list(refs)
    x = refs.pop(0)[...]
    if mixer:
        y = refs.pop(0)[...]
        if gated:
            y = y * refs.pop(0)[...]
        x = x + _dot(y.astype(BF16), refs.pop(0)[...])
    g_ref, wg_ref, wu_ref, wd_ref, o_ref = refs
    h = _rms(x, g_ref[...]).astype(BF16)
    gate = _dot(h, wg_ref[...])
    up = _dot(h, wu_ref[...])
    act = (gate * _sigmoid(gate) * up).astype(BF16)
    o_ref[...] = x + 0.5 * _dot(act, wd_ref[...])


def _resident(shape):
    return pl.BlockSpec(shape, lambda *_: (0,) * len(shape), pipeline_mode=pl.Buffered(1))


def _ffn(x, g, wg, wu, wd, *, mixer=None, tm=512):
    T, D = x.shape
    tm = min(tm, T)
    tile = pl.BlockSpec((tm, D), lambda i: (i, 0))
    ins, specs = [x], [tile]
    if mixer is not None:
        y, y_gate, w_o = mixer
        ins += [y] + ([y_gate] if y_gate is not None else []) + [w_o.astype(BF16)]
        specs += [tile] * (len(ins) - 2) + [_resident(w_o.shape)]
    ins += [g.reshape(1, D), wg.astype(BF16), wu.astype(BF16), wd.astype(BF16)]
    specs += [_resident((1, D)), _resident(wg.shape), _resident(wu.shape), _resident(wd.shape)]
    return pl.pallas_call(
        functools.partial(_ffn_kernel, mixer=mixer is not None, gated=mixer is not None and mixer[1] is not None),
        grid=(T // tm,),
        in_specs=specs,
        out_specs=tile,
        out_shape=jax.ShapeDtypeStruct((T, D), F32),
        compiler_params=_cparams(("parallel",)),
        name="ffn",
    )(*ins)


def _rwkv_prep_kernel(x_ref, xp_ref, gm_ref, mu_ref, wrkv_ref, w0_ref, wla_ref, wlb_ref, a0_ref,
                      ala_ref, alb_ref, gla_ref, glb_ref,
                      r_ref, lw_ref, k_ref, v_ref, a_ref, g_ref, *, tiles_per_seq):
    i = pl.program_id(0)
    gm = gm_ref[...]
    h = _rms(x_ref[...], gm)
    hp = _rms(xp_ref[...], gm)[SUBLANES - 1:SUBLANES, :]
    hp = jnp.where(i % tiles_per_seq == 0, 0.0, hp)
    row = lax.broadcasted_iota(jnp.int32, h.shape, 0)
    prev = jnp.where(row == 0, hp, pltpu.roll(h, 1, axis=0))
    xx = prev - h
    mu = mu_ref[...]
    mix = lambda c: (h + xx * mu[c:c + 1, :]).astype(BF16)
    r_ref[...] = _dot(mix(0), wrkv_ref[0])
    k_ref[...] = _dot(mix(2), wrkv_ref[1])
    v_ref[...] = _dot(mix(3), wrkv_ref[2])
    z = w0_ref[...] + _dot(jnp.tanh(_dot(mix(1), wla_ref[...])).astype(BF16), wlb_ref[...])
    lw_ref[...] = -float(np.exp(-0.5)) * _sigmoid(z)
    a_ref[...] = _sigmoid(a0_ref[...] + _dot(_dot(mix(4), ala_ref[...]).astype(BF16), alb_ref[...]))
    g_ref[...] = _dot(_sigmoid(_dot(mix(5), gla_ref[...])).astype(BF16), glb_ref[...])


def _rwkv_prep(x, gmix, mu, w_rkv, w0, wla, wlb, a0, ala, alb, gla, glb, *, seq, tm=512):
    T, D = x.shape
    tm = min(tm, seq)
    full = lambda a: _resident(a.shape)
    row = lambda a: a.reshape(1, D)
    args = [gmix.reshape(1, D), mu, w_rkv.astype(BF16), row(w0), wla.astype(BF16), wlb.astype(BF16),
            row(a0), ala.astype(BF16), alb.astype(BF16), gla.astype(BF16), glb.astype(BF16)]
    out = jax.ShapeDtypeStruct((T, D), F32)
    return pl.pallas_call(
        functools.partial(_rwkv_prep_kernel, tiles_per_seq=seq // tm),
        grid=(T // tm,),
        in_specs=[pl.BlockSpec((tm, D), lambda i: (i, 0)),
                  pl.BlockSpec((SUBLANES, D), lambda i: (jnp.maximum(i * (tm // SUBLANES) - 1, 0), 0))]
                 + [full(a) for a in args],
        out_specs=[pl.BlockSpec((tm, D), lambda i: (i, 0))] * 6,
        out_shape=[out] * 6,
        compiler_params=_cparams(("parallel",)),
        name="rwkv_prep",
    )(x, x, *args)


PAIR = 2 * RW_HEAD_DIM


def _blockdiag(x, masks):
    return jnp.concatenate([x * masks[0], x * masks[1]], axis=0)


def _pair_dot(lhs, rhs_list, masks, transpose_rhs=False):
    dot = _dot_nt if transpose_rhs else _dot
    rhs = jnp.concatenate([_blockdiag(r.astype(BF16), masks) for r in rhs_list], axis=0 if transpose_rhs else 1)
    out = dot(lhs.astype(BF16), rhs)
    return [out[:, i * PAIR:(i + 1) * PAIR] for i in range(len(rhs_list))]


def _seg_sum(x, is_a):
    zero = jnp.zeros_like(x)
    sa = jnp.sum(jnp.where(is_a, x, zero), axis=-1, keepdims=True)
    sb = jnp.sum(jnp.where(is_a, zero, x), axis=-1, keepdims=True)
    return jnp.where(is_a, sa, sb)


def _rwkv_chunk(r, lw, k, v, a, kk_p, ka_p, rk_p, gn_w, gn_b, state, consts):
    is_a, masks, tri_incl, tri_strict, blockdiag, eye, ltri = consts
    inv_n = 1.0 / RW_HEAD_DIM
    kkr = k * kk_p
    kk = kkr / jnp.maximum(jnp.sqrt(_seg_sum(kkr * kkr, is_a)), 1e-12)
    k2 = k * (1.0 + (a - 1.0) * ka_p)
    kb = kk * a

    cum = _sel_left(ltri, lw)
    yield
    cum_last = cum[CHUNK - 1:CHUNK, :]
    einv = jnp.exp(-cum)
    elast = jnp.exp(cum_last - cum)
    a_t = -kk * jnp.exp(cum - lw)
    r_t = r * jnp.exp(cum)
    b_h = kb * einv
    k_h = k2 * einv
    b_l = kb * elast
    k_l = k2 * elast

    ar = jnp.concatenate([a_t, r_t], axis=0)
    d1 = lambda p, qs: _pair_dot(p, qs, masks)
    sc_b, sc_k = _pair_dot(ar, [b_h, k_h], masks, transpose_rhs=True)
    yield
    m_ab = jnp.where(tri_strict, sc_b[:CHUNK], 0.0)
    m_rb = jnp.where(tri_incl, sc_b[CHUNK:], 0.0)
    m_ak = jnp.where(tri_strict, sc_k[:CHUNK], 0.0)
    m_rk = jnp.where(tri_incl, sc_k[CHUNK:], 0.0)

    (w1,) = d1(m_ak, [v])
    m_d = jnp.where(blockdiag, m_ab, 0.0)
    m_o = m_ab - m_d
    dinv = eye + m_d
    pw = m_d
    for _ in range(int(np.log2(INV_BLOCK)) - 1):
        (pw,) = d1(pw, [pw])
        yield
        dinv = dinv + d1(dinv, [pw])[0]
        yield
    n1, x1, x2 = d1(dinv, [m_o, a_t, w1])
    yield
    (n2,) = d1(n1, [n1])
    yield
    u1, u2 = d1(n2, [x1, x2])
    x1, x2 = x1 + u1, x2 + u2
    yield
    u1, u2 = d1(n1, [x1, x2])
    x1, x2 = x1 + u1, x2 + u2
    yield

    (z_g,) = d1(m_rb, [x1])
    g1 = r_t + z_g
    x2b, vb = x2.astype(BF16), v.astype(BF16)
    y_c = _dot(jnp.concatenate([m_rb, m_rk], axis=1).astype(BF16),
               jnp.concatenate([_blockdiag(x2b, masks), _blockdiag(vb, masks)], axis=0))
    z2 = _dot_tn(jnp.concatenate([b_l, k_l], axis=0).astype(BF16),
                 jnp.concatenate([jnp.concatenate([x1.astype(BF16), x2b], axis=1),
                                  jnp.concatenate([jnp.zeros_like(vb), vb], axis=1)], axis=0))
    yield
    diag_blocks = lambda f: jnp.where(is_a, f[:RW_HEAD_DIM], f[RW_HEAD_DIM:])
    trans = jnp.where(eye > 0, jnp.exp(cum_last), 0.0) + diag_blocks(z2[:, :PAIR])
    add = diag_blocks(z2[:, PAIR:])
    (ys,) = d1(jnp.concatenate([g1, trans], axis=0), [state])
    yield
    y = ys[:CHUNK] + y_c
    new_state = ys[CHUNK:] + add

    mean = _seg_sum(y, is_a) * inv_n
    dev = y - mean
    var = _seg_sum(dev * dev, is_a) * inv_n
    yn = dev * lax.rsqrt(var + RW_GN_EPS) * gn_w + gn_b
    bonus = _seg_sum(r * k2 * rk_p, is_a) * v
    return yn + bonus, new_state


def _rwkv_scan_kernel(r_ref, lw_ref, k_ref, v_ref, a_ref, kk_ref, ka_ref, rk_ref, gw_ref, gb_ref, o_ref,
                      state_ref, *, n_chunks, pairs):
    @pl.when(pl.program_id(2) == 0)
    def _():
        state_ref[...] = jnp.zeros_like(state_ref)

    row = lax.broadcasted_iota(jnp.int32, (CHUNK, PAIR), 0)
    lane = lax.broadcasted_iota(jnp.int32, (CHUNK, PAIR), 1)
    col = lane & (RW_HEAD_DIM - 1)
    is_a = lane < RW_HEAD_DIM
    blockdiag = (row // INV_BLOCK) == (col // INV_BLOCK)
    eye = jnp.where(row == col, 1.0, 0.0).astype(F32)
    row_s = lax.broadcasted_iota(jnp.int32, (CHUNK, CHUNK), 0)
    col_s = lax.broadcasted_iota(jnp.int32, (CHUNK, CHUNK), 1)
    ltri = jnp.where(row_s >= col_s, 1.0, 0.0).astype(BF16)
    masks = (jnp.where(is_a, 1.0, 0.0).astype(BF16), jnp.where(is_a, 0.0, 1.0).astype(BF16))
    consts = (is_a, masks, row >= col, row > col, blockdiag, eye, ltri)

    chains = [(s, p) for s in range(r_ref.shape[0]) for p in range(pairs)]

    def body(c, carry):
        t0 = pl.multiple_of(c * CHUNK, CHUNK)
        gens = []
        for s, p in chains:
            sl = slice(p * PAIR, (p + 1) * PAIR)
            ld = lambda ref: ref[s, pl.ds(t0, CHUNK), sl]
            gens.append(_rwkv_chunk(ld(r_ref), ld(lw_ref), ld(k_ref), ld(v_ref), ld(a_ref),
                                    kk_ref[:, sl], ka_ref[:, sl], rk_ref[:, sl], gw_ref[:, sl], gb_ref[:, sl],
                                    state_ref[s * pairs + p], consts))
        results = [None] * len(chains)
        while any(res is None for res in results):
            for i, gen in enumerate(gens):
                if results[i] is None:
                    try:
                        next(gen)
                    except StopIteration as done:
                        results[i] = done.value
        for (s, p), (y, st) in zip(chains, results):
            state_ref[s * pairs + p] = st
            o_ref[s, pl.ds(t0, CHUNK), p * PAIR:(p + 1) * PAIR] = y
        return carry

    lax.fori_loop(0, n_chunks, body, 0)


def _rwkv_scan(r, lw, k, v, a, k_k, k_a, r_k, gn_w, gn_b, *, batch, seq, pairs=8, seqs=2, tt=256):
    T, D = r.shape
    tt = min(tt, seq)
    seqs = min(seqs, batch)
    lanes = pairs * PAIR
    seq_spec = pl.BlockSpec((seqs, tt, lanes), lambda b, p, t: (b, t, p))
    par_spec = pl.BlockSpec((1, lanes), lambda b, p, t: (0, p))
    as3 = lambda t: t.reshape(batch, seq, D)
    row = lambda t: t.reshape(1, D)
    y = pl.pallas_call(
        functools.partial(_rwkv_scan_kernel, n_chunks=tt // CHUNK, pairs=pairs),
        grid=(batch // seqs, D // lanes, seq // tt),
        in_specs=[seq_spec] * 5 + [par_spec] * 5,
        out_specs=seq_spec,
        out_shape=jax.ShapeDtypeStruct((batch, seq, D), F32),
        scratch_shapes=[pltpu.VMEM((seqs * pairs, RW_HEAD_DIM, PAIR), F32)],
        compiler_params=_cparams(("parallel", "parallel", "arbitrary")),
        name="rwkv_scan",
    )(as3(r), as3(lw), as3(k), as3(v), as3(a), row(k_k), row(k_a), row(r_k), row(gn_w), row(gn_b))
    return y.reshape(T, D)


def _group_rms(x, gain, pool, expand):
    pooled = _dot((x * x).astype(BF16), pool)
    inv = lax.rsqrt(pooled * (1.0 / HEAD_DIM) + NORM_EPS)
    hi = inv.astype(BF16)
    lo = (inv - hi.astype(F32)).astype(BF16)
    return x * (_dot(hi, expand) + _dot(lo, expand)) * gain


def _nsa_kv_kernel(x_ref, gn_ref, w_ref, kn_ref, cos_ref, sin_ref, pool_ref, exp_ref,
                   kc_ref, vc_ref, ks_ref, vs_ref, kw_ref, vw_ref):
    h = _rms(x_ref[...], gn_ref[...]).astype(BF16)
    kv = _dot(h, w_ref[...])
    W = NSA_GROUPS * HEAD_DIM
    part = lambda i: kv[:, i * W:(i + 1) * W]
    cos, sin = cos_ref[...], sin_ref[...]
    pool, expand = pool_ref[...], exp_ref[...]

    def put(ref, val):
        for g in range(NSA_GROUPS):
            ref[g] = val[:, g * HEAD_DIM:(g + 1) * HEAD_DIM].astype(ref.dtype)

    put(kc_ref, part(0))
    put(vc_ref, part(1))
    put(ks_ref, _rope(_group_rms(part(2), kn_ref[1:2, :], pool, expand), cos, sin))
    put(vs_ref, part(3))
    put(kw_ref, _rope(_group_rms(part(4), kn_ref[2:3, :], pool, expand), cos, sin))
    put(vw_ref, part(5))


def _pool_matrices(width):
    head = np.arange(width) // HEAD_DIM
    pool = (head[:, None] == np.arange(LANES)[None, :]).astype(np.float32)
    return jnp.asarray(pool, BF16), jnp.asarray(pool.T, BF16)


def _rope_tables(seq, width):
    half = HEAD_DIM // 2
    inv = ROPE_THETA ** (-jnp.arange(half, dtype=F32) / half)
    ang = jnp.arange(seq).astype(F32)[:, None] * inv[None, :]
    cos, sin = jnp.cos(ang), jnp.sin(ang)
    reps = width // HEAD_DIM
    return (jnp.tile(jnp.concatenate([cos, cos], axis=1), (1, reps)),
            jnp.tile(jnp.concatenate([-sin, sin], axis=1), (1, reps)))


def _nsa_kv(x, kv_norm, kv_w, kv_k_norm, *, batch, seq, tm=512):
    T, D = x.shape
    tm = min(tm, seq)
    G, dh = NSA_GROUPS, HEAD_DIM
    W = G * dh
    cos, sin = _rope_tables(seq, W)
    pool, expand = _pool_matrices(W)
    kn = jnp.tile(kv_k_norm, (1, G))
    tps = seq // tm
    full = lambda a: pl.BlockSpec(a.shape, lambda b, t: (0,) * a.ndim)
    tab = pl.BlockSpec((tm, W), lambda b, t: (t, 0))
    out_tile = pl.BlockSpec((None, G, tm, dh), lambda b, t: (b, 0, t, 0))
    w = kv_w.astype(BF16)
    shape = (batch, G, seq, dh)
    return pl.pallas_call(
        _nsa_kv_kernel,
        grid=(batch, tps),
        in_specs=[pl.BlockSpec((tm, D), lambda b, t: (b * tps + t, 0)), pl.BlockSpec((1, D), lambda b, t: (0, 0)),
                  full(w), full(kn), tab, tab, full(pool), full(expand)],
        out_specs=[out_tile] * 6,
        out_shape=[jax.ShapeDtypeStruct(shape, F32)] * 2 + [jax.ShapeDtypeStruct(shape, BF16)] * 4,
        compiler_params=_cparams(("parallel", "parallel")),
        name="nsa_kv",
    )(x, kv_norm.reshape(1, D), w, kn, cos, sin, pool, expand)


def _compress(tok_ref, pos_ref, w1_ref, w2_ref):
    nch = tok_ref.shape[0] // D_CMP
    dh = tok_ref.shape[1]
    lo = hi = None
    for l in range(D_CMP):
        rows = tok_ref[pl.ds(l, nch, stride=D_CMP), :]
        for half in range(L_CMP // D_CMP):
            p = half * D_CMP + l
            term = _dot((rows + pos_ref[p:p + 1, :]).astype(BF16), w1_ref[p * dh:(p + 1) * dh, :])
            if half == 0:
                lo = term if lo is None else lo + term
            else:
                hi = term if hi is None else hi + term
    hid = lo + pltpu.roll(hi, nch - 1, axis=0)
    act = 0.5 * hid * (1.0 + jnp.tanh(np.sqrt(2.0 / np.pi) * (hid + 0.044715 * hid * hid * hid)))
    return _dot(act.astype(BF16), w2_ref[...])


def _nsa_cmp_kernel(ck_ref, cv_ref, pk_ref, pv_ref, w1k_ref, w2k_ref, w1v_ref, w2v_ref, kn_ref,
                    ko_ref, vo_ref):
    ko_ref[...] = _rms(_compress(ck_ref, pk_ref, w1k_ref, w2k_ref), kn_ref[...])
    vo_ref[...] = _compress(cv_ref, pv_ref, w1v_ref, w2v_ref)


def _nsa_cmp(kc_tok, vc_tok, kn0, pos_k, w1k, w2k, pos_v, w1v, w2v, *, batch, seq):
    G, dh = NSA_GROUPS, HEAD_DIM
    assert L_CMP == 2 * D_CMP
    nch = seq // D_CMP
    full = lambda a: pl.BlockSpec(a.shape, lambda b, g: (0,) * a.ndim)
    blk = pl.BlockSpec((None, None, seq, dh), lambda b, g: (b, g, 0, 0))
    oblk = pl.BlockSpec((None, nch, dh), lambda b, g: (b * G + g, 0, 0))
    args = [pos_k, pos_v, w1k.astype(BF16), w2k.astype(BF16), w1v.astype(BF16), w2v.astype(BF16),
            kn0.reshape(1, dh)]
    return pl.pallas_call(
        _nsa_cmp_kernel,
        grid=(batch, G),
        in_specs=[blk, blk] + [full(a) for a in args],
        out_specs=[oblk, oblk],
        out_shape=[jax.ShapeDtypeStruct((batch * G, nch, dh), F32)] * 2,
        compiler_params=_cparams(("parallel", "parallel")),
        name="nsa_cmp",
    )(kc_tok, vc_tok, *args)


def _nsa_q_kernel(x_ref, gn_ref, wq_ref, wg_ref, qn_ref, cos_ref, sin_ref, pool_ref, exp_ref,
                  q_ref, qr_ref, gate_ref):
    h = _rms(x_ref[...], gn_ref[...]).astype(BF16)
    q = _group_rms(_dot(h, wq_ref[...]), qn_ref[...], pool_ref[...], exp_ref[...])
    reps = q.shape[1] // cos_ref.shape[1]
    cos = jnp.concatenate([cos_ref[...]] * reps, axis=1)
    sin = jnp.concatenate([sin_ref[...]] * reps, axis=1)
    scale = LOGIT_SCALE
    q_ref[...] = (q * scale).astype(BF16)
    qr_ref[...] = (_rope(q, cos, sin) * scale).astype(BF16)
    gate_ref[...] = _sigmoid(_dot(h, wg_ref[...]))


def _nsa_q(x, gmix, w_q, q_norm, *, seq, tm=512):
    T, D = x.shape
    tm = min(tm, seq)
    HD = NSA_HEADS * HEAD_DIM
    cos, sin = _rope_tables(seq, LANES)
    pool, expand = _pool_matrices(HD)
    wq = w_q[:, :HD].astype(BF16)
    wg = jnp.pad(w_q[:, HD:], ((0, 0), (0, LANES - 3 * NSA_HEADS))).astype(BF16)
    qn = jnp.tile(q_norm.reshape(1, HEAD_DIM), (1, NSA_HEADS))
    tps = seq // tm
    full = lambda a: pl.BlockSpec(a.shape, lambda i: (0,) * a.ndim)
    tab = pl.BlockSpec((tm, LANES), lambda i: (i % tps, 0))
    tile = lambda w: pl.BlockSpec((tm, w), lambda i: (i, 0))
    return pl.pallas_call(
        _nsa_q_kernel,
        grid=(T // tm,),
        in_specs=[tile(D), pl.BlockSpec((1, D), lambda i: (0, 0)), full(wq), full(wg), full(qn),
                  tab, tab, full(pool), full(expand)],
        out_specs=[tile(HD), tile(HD), tile(LANES)],
        out_shape=[jax.ShapeDtypeStruct((T, HD), BF16)] * 2 + [jax.ShapeDtypeStruct((T, LANES), F32)],
        compiler_params=_cparams(("parallel",)),
        name="nsa_q",
    )(x, gmix.reshape(1, D), wq, wg, qn, cos, sin, pool, expand)


RANK_ROWS = SUBLANES
MAX_SAFE_SHIFT = 43.0


def _topk_bias(score, n_top):
    nsel = score.shape[0]
    groups = [score[lo:lo + RANK_ROWS, :] for lo in range(0, nsel, RANK_ROWS)]
    ranks = [jnp.zeros(g.shape, jnp.int32) for g in groups]
    for i in range(nsel):
        si = score[i:i + 1, :]
        for gi, sg in enumerate(groups):
            lo = gi * RANK_ROWS
            if lo > i:
                beats = si >= sg
            elif lo + RANK_ROWS - 1 < i:
                beats = si > sg
            else:
                later = lax.broadcasted_iota(jnp.int32, sg.shape, 0) + lo > i
                beats = (si > sg) | ((si == sg) & later)
            ranks[gi] = ranks[gi] + jnp.where(beats, 1, 0)
    return [jnp.where(r < n_top, 0.0, NEG_INF) for r in ranks]


def _nsa_attn_kernel(*refs):
    bound_ref = refs[11]
    safe = ((bound_ref[0] <= MAX_SAFE_SHIFT) & (bound_ref[1] <= MAX_SAFE_SHIFT)
            & (bound_ref[2] <= MAX_SAFE_SHIFT))
    pl.when(safe)(functools.partial(_nsa_attn_step, *refs, use_bound=True))
    pl.when(jnp.logical_not(safe))(functools.partial(_nsa_attn_step, *refs, use_bound=False))


def _nsa_attn_step(q_ref, qr_ref, gate_ref, kc_ref, vc_ref, ks_ref, vs_ref, kw_ref, vw_ref, msel_ref,
                   wbias_ref, bound_ref, o_ref, bias_ref, *, use_bound):
    TQ, HPG, dh = Q_TILE, HEADS_PER_GROUP, HEAD_DIM
    NQ = HPG * TQ
    qi = pl.program_id(2)
    s0 = qi * TQ
    heads = lambda ref: [ref[:, hh * dh:(hh + 1) * dh] for hh in range(HPG)]
    q_heads, qr_heads = heads(q_ref), heads(qr_ref)
    qk = lambda keys, qs: jnp.concatenate([_dot_nt(keys, qh) for qh in qs], axis=1)
    over_heads = lambda a: jnp.concatenate([a] * HPG, axis=1)
    rows = lambda shape: lax.broadcasted_iota(jnp.int32, shape, 0)
    cols = lambda shape: lax.broadcasted_iota(jnp.int32, shape, 1)

    def softmax_then(masked_logits, bound, finish):
        if use_bound:
            e = jnp.exp2(masked_logits(bound))
        else:
            s = masked_logits(0.0)
            e = jnp.exp2(s - jnp.max(s, axis=0, keepdims=True))
        return finish(e, jnp.sum(e, axis=0, keepdims=True))

    ncp = kc_ref.shape[0]
    shape = (ncp, TQ)
    cmp_bias = jnp.where(rows(shape) * D_CMP + (L_CMP - 1) <= s0 + cols(shape), 0.0, NEG_INF)
    sees_any = s0 + (cols((1, NQ)) & (TQ - 1)) >= L_CMP - 1

    def cmp_finish(e, l):
        p = e * jnp.where(sees_any, 1.0 / l, 0.0)
        imp = p[:, 0:TQ]
        for hh in range(1, HPG):
            imp = imp + p[:, hh * TQ:(hh + 1) * TQ]
        return _dot_tn(vc_ref[...].astype(BF16), p.astype(BF16)), imp

    o_cmp, imp = softmax_then(lambda shift: qk(kc_ref[...].astype(BF16), q_heads) + over_heads(cmp_bias - shift),
                              bound_ref[0], cmp_finish)

    p_slc = _sel_left(msel_ref[...], imp)
    nsel = p_slc.shape[0]
    shape = (nsel, TQ)
    blk = rows(shape)
    t = s0 + cols(shape)
    cur = t >> int(np.log2(L_SEL))
    forced = (blk == 0) | (blk == cur) | (blk == cur - 1)
    score = jnp.where(forced, FORCE_SCORE, p_slc)
    score = jnp.where(blk * L_SEL > t, -1.0, score)
    for gi, group_bias in enumerate(_topk_bias(score, min(N_SEL, nsel))):
        bias_ref[gi] = group_bias

    KT = RANK_ROWS * L_SEL
    last = (qi * TQ) // KT
    shape = (KT, TQ)
    causal_bias = jnp.where(rows(shape) <= cols(shape) + (s0 - last * KT), 0.0, NEG_INF)

    def tile_logits(j):
        return qk(ks_ref[pl.ds(pl.multiple_of(j * KT, KT), KT), :], qr_heads)

    def sel_slabs(logits, j, diagonal, shift):
        bias = bias_ref[j]
        for b in range(RANK_ROWS):
            blk_rows = slice(b * L_SEL, (b + 1) * L_SEL)
            mask = bias[b:b + 1, :] + causal_bias[blk_rows, :] if diagonal else bias[b:b + 1, :]
            yield logits[blk_rows, :] + (over_heads(mask) - shift)

    def sel_max(j, m, diagonal):
        for s in sel_slabs(tile_logits(j), j, diagonal, 0.0):
            m = jnp.maximum(m, jnp.max(s, axis=0, keepdims=True))
        return m

    def true_max():
        m = lax.fori_loop(0, last, lambda j, m: sel_max(j, m, False), jnp.full((1, NQ), NEG_INF, F32))
        return sel_max(last, m, True)

    m_sel = jnp.full((1, NQ), bound_ref[1], F32) if use_bound else true_max()

    def sel_accumulate(tiles, carry, diagonal):
        l, acc = carry
        for j, logits in [(j, tile_logits(j)) for j in tiles]:
            ps = []
            for s in sel_slabs(logits, j, diagonal, m_sel):
                p = jnp.exp2(s)
                l = l + jnp.sum(p.reshape(L_SEL // SUBLANES, SUBLANES, NQ), axis=0)
                ps.append(p.astype(BF16))
            acc = acc + _dot_tn(vs_ref[pl.ds(pl.multiple_of(j * KT, KT), KT), :], jnp.concatenate(ps, axis=0))
        return l, acc

    carry = (jnp.zeros((SUBLANES, NQ), F32), jnp.zeros((dh, NQ), F32))
    carry = lax.fori_loop(0, last // 2, lambda i, c: sel_accumulate([2 * i, 2 * i + 1], c, False), carry)
    carry = lax.fori_loop(2 * (last // 2), last, lambda j, c: sel_accumulate([j], c, False), carry)
    l_sel, acc_sel = sel_accumulate([last], carry, True)
    o_sel = acc_sel / jnp.sum(l_sel, axis=0, keepdims=True)

    span = WINDOW + TQ
    start = pl.multiple_of(jnp.maximum(s0 - WINDOW, 0), TQ)
    o_win = softmax_then(
        lambda shift: qk(kw_ref[pl.ds(start, span), :], qr_heads) + over_heads(wbias_ref[...] - shift),
        bound_ref[2], lambda e, l: _dot_tn(vw_ref[pl.ds(start, span), :], e.astype(BF16)) / l)

    outs = []
    for hh in range(HPG):
        sl = slice(hh * TQ, (hh + 1) * TQ)
        g = lambda c: gate_ref[3 * hh + c:3 * hh + c + 1, :]
        o_h = g(0) * o_cmp[:, sl] + g(1) * o_sel[:, sl] + g(2) * o_win[:, sl]
        outs.append(o_h.T)
    o_ref[...] = jnp.concatenate(outs, axis=1)


def _selection_matrix(seq):
    nsel, ncp = seq // L_SEL, seq // D_CMP
    stride = L_SEL // D_CMP
    lpad = L_CMP // D_CMP - 1
    n = np.arange(ncp)[None, :]
    j = np.arange(nsel)[:, None]
    m = (n >= stride * j - lpad) & (n <= stride * j + stride - 1)
    return jnp.asarray(m.astype(np.float32), BF16)


def _window_bias():
    r = np.arange(WINDOW + Q_TILE)[:, None]
    c = np.arange(Q_TILE)[None, :]
    offs = [min(i * Q_TILE, WINDOW) for i in range(WINDOW // Q_TILE + 1)]
    masks = [np.where((r <= c + off) & (r > c + off - WINDOW), 0.0, NEG_INF) for off in offs]
    return jnp.asarray(np.stack(masks), F32)


def _nsa_attn(q, qr, gates, bounds, k_cmp, v_cmp, k_sel, v_sel, k_win, v_win, *, batch, seq):
    G, HPG, dh = NSA_GROUPS, HEADS_PER_GROUP, HEAD_DIM
    T = batch * seq
    nt = seq // Q_TILE
    gw = G * HPG * 3
    gt = jnp.transpose(gates[:, :gw].reshape(batch, seq, G, HPG * 3), (0, 2, 3, 1))
    gt = jnp.pad(gt, ((0, 0), (0, 0), (0, GATE_ROWS - HPG * 3), (0, 0)))
    ncp = seq // D_CMP
    msel = _selection_matrix(seq)
    wbias = _window_bias()
    key_tile = RANK_ROWS * L_SEL
    assert seq % key_tile == 0 and key_tile % Q_TILE == 0
    qspec = pl.BlockSpec((Q_TILE, HPG * dh), lambda b, g, i: (b * nt + i, g))
    cspec = pl.BlockSpec((None, ncp, dh), lambda b, g, i: (b * G + g, 0, 0))
    kspec = pl.BlockSpec((None, None, seq, dh), lambda b, g, i: (b, g, 0, 0))
    return pl.pallas_call(
        _nsa_attn_kernel,
        grid=(batch, G, nt),
        in_specs=[qspec, qspec, pl.BlockSpec((None, None, GATE_ROWS, Q_TILE), lambda b, g, i: (b, g, 0, i)),
                  cspec, cspec, kspec, kspec, kspec, kspec,
                  pl.BlockSpec(msel.shape, lambda b, g, i: (0, 0)),
                  pl.BlockSpec((None,) + wbias.shape[1:],
                               lambda b, g, i: (jnp.minimum(i, wbias.shape[0] - 1), 0, 0)),
                  pl.BlockSpec(memory_space=pltpu.SMEM)],
        out_specs=qspec,
        out_shape=jax.ShapeDtypeStruct((T, G * HPG * dh), F32),
        scratch_shapes=[pltpu.VMEM((seq // key_tile, RANK_ROWS, Q_TILE), F32)],
        compiler_params=_cparams(("parallel", "parallel", "arbitrary")),
        name="nsa_attn",
    )(q, qr, gt, k_cmp, v_cmp, k_sel, v_sel, k_win, v_win, msel, wbias, bounds)


def _logit_bound(q_gain, k_gain):
    rounding = 1.02
    return (HEAD_DIM * LOGIT_SCALE * rounding) * jnp.max(jnp.abs(q_gain)) * jnp.max(jnp.abs(k_gain))


def kernel(x, norm_ffn1, ffn1_w_gate, ffn1_w_up, ffn1_w_down, norm_mix, norm_ffn2, ffn2_w_gate, ffn2_w_up, ffn2_w_down, rw_mu, rw_w_rkv, rw_w0, rw_w_lora_a, rw_w_lora_b, rw_a0, rw_a_lora_a, rw_a_lora_b, rw_g_lora_a, rw_g_lora_b, rw_k_k, rw_k_a, rw_r_k, rw_gn_w, rw_gn_b, rw_w_o, kv_norm, kv_w, kv_k_norm, cmp_pos_k, cmp_k_w1, cmp_k_w2, cmp_pos_v, cmp_v_w1, cmp_v_w2, nsa_w_q, nsa_q_norm, nsa_w_o):
    B, S, D = x.shape
    depth = norm_ffn1.shape[0]
    n_rwkv = rw_mu.shape[0]
    assert S % max(Q_TILE, CHUNK) == 0 and S >= WINDOW + Q_TILE and D % LANES == 0
    xt = x.reshape(B * S, D)
    shared = None
    for l in range(depth):
        xt = _ffn(xt, norm_ffn1[l], ffn1_w_gate[l], ffn1_w_up[l], ffn1_w_down[l])
        if l < n_rwkv:
            i = l
            r, lw, k, v, a, g = _rwkv_prep(
                xt, norm_mix[l], rw_mu[i], rw_w_rkv[i], rw_w0[i], rw_w_lora_a[i], rw_w_lora_b[i], rw_a0[i],
                rw_a_lora_a[i], rw_a_lora_b[i], rw_g_lora_a[i], rw_g_lora_b[i], seq=S)
            y = _rwkv_scan(r, lw, k, v, a, rw_k_k[i], rw_k_a[i], rw_r_k[i], rw_gn_w[i], rw_gn_b[i],
                           batch=B, seq=S)
            mixer = (y, g, rw_w_o[i])
        else:
            i = l - n_rwkv
            q, qr, gates = _nsa_q(xt, norm_mix[l], nsa_w_q[i], nsa_q_norm[i], seq=S)
            bounds = jnp.stack([_logit_bound(nsa_q_norm[i], kv_k_norm[br]) for br in range(3)])
            o = _nsa_attn(q, qr, gates, bounds, *shared, batch=B, seq=S)
            mixer = (o, None, nsa_w_o[i])
        xt = _ffn(xt, norm_ffn2[l], ffn2_w_gate[l], ffn2_w_up[l], ffn2_w_down[l], mixer=mixer)
        if l == n_rwkv - 1:
            kc_tok, vc_tok, k_sel, v_sel, k_win, v_win = _nsa_kv(xt, kv_norm, kv_w, kv_k_norm, batch=B, seq=S)
            k_cmp, v_cmp = _nsa_cmp(kc_tok, vc_tok, kv_k_norm[0], cmp_pos_k, cmp_k_w1, cmp_k_w2,
                                    cmp_pos_v, cmp_v_w1, cmp_v_w2, batch=B, seq=S)
            shared = (k_cmp, v_cmp, k_sel, v_sel, k_win, v_win)
    return xt.reshape(B, S, D)
```

```python
import functools

import numpy as np
import jax
import jax.numpy as jnp
from jax import lax
from jax.experimental import pallas as pl
from jax.experimental.pallas import tpu as pltpu

F32 = jnp.float32
BF16 = jnp.bfloat16

NORM_EPS = 1e-6
RW_HEAD_DIM = 64
RW_GN_EPS = 64e-5
NSA_HEADS = 16
NSA_GROUPS = 4
HEADS_PER_GROUP = NSA_HEADS // NSA_GROUPS
HEAD_DIM = 64
L_CMP = 32
D_CMP = 16
L_SEL = 64
N_SEL = 16
WINDOW = 512
ROPE_THETA = 10000.0
FORCE_SCORE = 1e6
NEG_INF = -1e30
LOGIT_SCALE = HEAD_DIM ** -0.5 * float(np.log2(np.e))

LANES = 128
SUBLANES = 8
PACKED_SUBLANES = 16
VMEM_LIMIT = 56 * 1024 * 1024

CHUNK = 64
INV_BLOCK = 16
Q_TILE = 256
GATE_ROWS = PACKED_SUBLANES


def _cparams(sem):
    return pltpu.CompilerParams(dimension_semantics=sem, vmem_limit_bytes=VMEM_LIMIT)


def _dot(a, b):
    return jnp.dot(a, b, preferred_element_type=F32)


def _dot_nt(a, b):
    return lax.dot_general(a, b, (((1,), (1,)), ((), ())), preferred_element_type=F32)


def _dot_tn(a, b):
    return lax.dot_general(a, b, (((0,), (0,)), ((), ())), preferred_element_type=F32)


def _split3(x):
    hi = x.astype(BF16)
    r1 = x - hi.astype(F32)
    mid = r1.astype(BF16)
    lo = (r1 - mid.astype(F32)).astype(BF16)
    return hi, mid, lo


def _sel_left(m01, x):
    x1, x2, x3 = _split3(x)
    return _dot(m01, x1) + (_dot(m01, x2) + _dot(m01, x3))


def _rms(x, g):
    return x * lax.rsqrt(jnp.mean(x * x, axis=-1, keepdims=True) + NORM_EPS) * g


def _sigmoid(z):
    return 1.0 / (1.0 + jnp.exp(-z))


def _rope(x, cos, sin_signed):
    n = x.shape[-1]
    half = HEAD_DIM // 2
    lane = lax.broadcasted_iota(jnp.int32, x.shape, 1)
    first = (lane & (HEAD_DIM - 1)) < half
    partner = jnp.where(first, pltpu.roll(x, n - half, axis=1), pltpu.roll(x, half, axis=1))
    return x * cos + partner * sin_signed


def _ffn_kernel(*refs, mixer, gated):
    refs = list(refs)
    x = refs.pop(0)[...]
    if mixer:
        y = refs.pop(0)[...]
        if gated:
            y = y * refs.pop(0)[...]
        x = x + _dot(y.astype(BF16), refs.pop(0)[...])
    g_ref, wg_ref, wu_ref, wd_ref, o_ref = refs
    h = _rms(x, g_ref[...]).astype(BF16)
    gate = _dot(h, wg_ref[...])
    up = _dot(h, wu_ref[...])
    act = (gate * _sigmoid(gate) * up).astype(BF16)
    o_ref[...] = x + 0.5 * _dot(act, wd_ref[...])


def _resident(shape):
    return pl.BlockSpec(shape, lambda *_: (0,) * len(shape), pipeline_mode=pl.Buffered(1))


def _ffn(x, g, wg, wu, wd, *, mixer=None, tm=512):
    T, D = x.shape
    tm = min(tm, T)
    tile = pl.BlockSpec((tm, D), lambda i: (i, 0))
    ins, specs = [x], [tile]
    if mixer is not None:
        y, y_gate, w_o = mixer
        ins += [y] + ([y_gate] if y_gate is not None else []) + [w_o.astype(BF16)]
        specs += [tile] * (len(ins) - 2) + [_resident(w_o.shape)]
    ins += [g.reshape(1, D), wg.astype(BF16), wu.astype(BF16), wd.astype(BF16)]
    specs += [_resident((1, D)), _resident(wg.shape), _resident(wu.shape), _resident(wd.shape)]
    return pl.pallas_call(
        functools.partial(_ffn_kernel, mixer=mixer is not None, gated=mixer is not None and mixer[1] is not None),
        grid=(T // tm,),
        in_specs=specs,
        out_specs=tile,
        out_shape=jax.ShapeDtypeStruct((T, D), F32),
        compiler_params=_cparams(("parallel",)),
        name="ffn",
    )(*ins)


def _rwkv_prep_kernel(x_ref, xp_ref, gm_ref, mu_ref, wrkv_ref, w0_ref, wla_ref, wlb_ref, a0_ref,
                      ala_ref, alb_ref, gla_ref, glb_ref,
                      r_ref, lw_ref, k_ref, v_ref, a_ref, g_ref, *, tiles_per_seq):
    i = pl.program_id(0)
    gm = gm_ref[...]
    h = _rms(x_ref[...], gm)
    hp = _rms(xp_ref[...], gm)[SUBLANES - 1:SUBLANES, :]
    hp = jnp.where(i % tiles_per_seq == 0, 0.0, hp)
    row = lax.broadcasted_iota(jnp.int32, h.shape, 0)
    prev = jnp.where(row == 0, hp, pltpu.roll(h, 1, axis=0))
    xx = prev - h
    mu = mu_ref[...]
    mix = lambda c: (h + xx * mu[c:c + 1, :]).astype(BF16)
    r_ref[...] = _dot(mix(0), wrkv_ref[0])
    k_ref[...] = _dot(mix(2), wrkv_ref[1])
    v_ref[...] = _dot(mix(3), wrkv_ref[2])
    z = w0_ref[...] + _dot(jnp.tanh(_dot(mix(1), wla_ref[...])).astype(BF16), wlb_ref[...])
    lw_ref[...] = -float(np.exp(-0.5)) * _sigmoid(z)
    a_ref[...] = _sigmoid(a0_ref[...] + _dot(_dot(mix(4), ala_ref[...]).astype(BF16), alb_ref[...]))
    g_ref[...] = _dot(_sigmoid(_dot(mix(5), gla_ref[...])).astype(BF16), glb_ref[...])


def _rwkv_prep(x, gmix, mu, w_rkv, w0, wla, wlb, a0, ala, alb, gla, glb, *, seq, tm=512):
    T, D = x.shape
    tm = min(tm, seq)
    full = lambda a: _resident(a.shape)
    row = lambda a: a.reshape(1, D)
    args = [gmix.reshape(1, D), mu, w_rkv.astype(BF16), row(w0), wla.astype(BF16), wlb.astype(BF16),
            row(a0), ala.astype(BF16), alb.astype(BF16), gla.astype(BF16), glb.astype(BF16)]
    out = jax.ShapeDtypeStruct((T, D), F32)
    return pl.pallas_call(
        functools.partial(_rwkv_prep_kernel, tiles_per_seq=seq // tm),
        grid=(T // tm,),
        in_specs=[pl.BlockSpec((tm, D), lambda i: (i, 0)),
                  pl.BlockSpec((SUBLANES, D), lambda i: (jnp.maximum(i * (tm // SUBLANES) - 1, 0), 0))]
                 + [full(a) for a in args],
        out_specs=[pl.BlockSpec((tm, D), lambda i: (i, 0))] * 6,
        out_shape=[out] * 6,
        compiler_params=_cparams(("parallel",)),
        name="rwkv_prep",
    )(x, x, *args)


PAIR = 2 * RW_HEAD_DIM


def _blockdiag(x, masks):
    return jnp.concatenate([x * masks[0], x * masks[1]], axis=0)


def _pair_dot(lhs, rhs_list, masks, transpose_rhs=False):
    dot = _dot_nt if transpose_rhs else _dot
    rhs = jnp.concatenate([_blockdiag(r.astype(BF16), masks) for r in rhs_list], axis=0 if transpose_rhs else 1)
    out = dot(lhs.astype(BF16), rhs)
    return [out[:, i * PAIR:(i + 1) * PAIR] for i in range(len(rhs_list))]


def _seg_sum(x, is_a):
    zero = jnp.zeros_like(x)
    sa = jnp.sum(jnp.where(is_a, x, zero), axis=-1, keepdims=True)
    sb = jnp.sum(jnp.where(is_a, zero, x), axis=-1, keepdims=True)
    return jnp.where(is_a, sa, sb)


def _rwkv_chunk(r, lw, k, v, a, kk_p, ka_p, rk_p, gn_w, gn_b, state, consts):
    is_a, masks, tri_incl, tri_strict, blockdiag, eye, ltri = consts
    inv_n = 1.0 / RW_HEAD_DIM
    kkr = k * kk_p
    kk = kkr / jnp.maximum(jnp.sqrt(_seg_sum(kkr * kkr, is_a)), 1e-12)
    k2 = k * (1.0 + (a - 1.0) * ka_p)
    kb = kk * a

    cum = _sel_left(ltri, lw)
    yield
    cum_last = cum[CHUNK - 1:CHUNK, :]
    einv = jnp.exp(-cum)
    elast = jnp.exp(cum_last - cum)
    a_t = -kk * jnp.exp(cum - lw)
    r_t = r * jnp.exp(cum)
    b_h = kb * einv
    k_h = k2 * einv
    b_l = kb * elast
    k_l = k2 * elast

    ar = jnp.concatenate([a_t, r_t], axis=0)
    d1 = lambda p, qs: _pair_dot(p, qs, masks)
    sc_b, sc_k = _pair_dot(ar, [b_h, k_h], masks, transpose_rhs=True)
    yield
    m_ab = jnp.where(tri_strict, sc_b[:CHUNK], 0.0)
    m_rb = jnp.where(tri_incl, sc_b[CHUNK:], 0.0)
    m_ak = jnp.where(tri_strict, sc_k[:CHUNK], 0.0)
    m_rk = jnp.where(tri_incl, sc_k[CHUNK:], 0.0)

    (w1,) = d1(m_ak, [v])
    m_d = jnp.where(blockdiag, m_ab, 0.0)
    m_o = m_ab - m_d
    dinv = eye + m_d
    (pw,) = d1(m_d, [m_d])
    yield
    for _ in range(int(np.log2(INV_BLOCK)) - 2):
        (both,) = d1(jnp.concatenate([pw, dinv], axis=0), [pw])
        pw, dinv = both[:CHUNK], dinv + both[CHUNK:]
        yield
    dinv = dinv + d1(dinv, [pw])[0]
    yield
    n1, x1, x2 = d1(dinv, [m_o, a_t, w1])
    yield
    n2, u1, u2 = d1(n1, [n1, x1, x2])
    x1, x2 = x1 + u1, x2 + u2
    yield
    u1, u2 = d1(n2, [x1, x2])
    x1, x2 = x1 + u1, x2 + u2
    yield

    x1b, x2b, vb = x1.astype(BF16), x2.astype(BF16), v.astype(BF16)
    bd_v = _blockdiag(vb, masks)
    zy = _dot(jnp.concatenate([m_rb, m_rk], axis=1).astype(BF16),
              jnp.concatenate([jnp.concatenate([_blockdiag(x1b, masks), _blockdiag(x2b, masks)], axis=1),
                               jnp.concatenate([jnp.zeros_like(bd_v), bd_v], axis=1)], axis=0))
    g1 = r_t + zy[:, :PAIR]
    y_c = zy[:, PAIR:]
    z2 = _dot_tn(jnp.concatenate([b_l, k_l], axis=0).astype(BF16),
                 jnp.concatenate([jnp.concatenate([x1b, x2b], axis=1),
                                  jnp.concatenate([jnp.zeros_like(vb), vb], axis=1)], axis=0))
    yield
    diag_blocks = lambda f: jnp.where(is_a, f[:RW_HEAD_DIM], f[RW_HEAD_DIM:])
    trans = jnp.where(eye > 0, jnp.exp(cum_last), 0.0) + diag_blocks(z2[:, :PAIR])
    add = diag_blocks(z2[:, PAIR:])
    (ys,) = d1(jnp.concatenate([g1, trans], axis=0), [state])
    yield
    y = ys[:CHUNK] + y_c
    new_state = ys[CHUNK:] + add

    mean = _seg_sum(y, is_a) * inv_n
    dev = y - mean
    var = _seg_sum(dev * dev, is_a) * inv_n
    yn = dev * lax.rsqrt(var + RW_GN_EPS) * gn_w + gn_b
    bonus = _seg_sum(r * k2 * rk_p, is_a) * v
    return yn + bonus, new_state


def _rwkv_scan_kernel(r_ref, lw_ref, k_ref, v_ref, a_ref, kk_ref, ka_ref, rk_ref, gw_ref, gb_ref, o_ref,
                      state_ref, *, n_chunks, pairs):
    @pl.when(pl.program_id(2) == 0)
    def _():
        state_ref[...] = jnp.zeros_like(state_ref)

    row = lax.broadcasted_iota(jnp.int32, (CHUNK, PAIR), 0)
    lane = lax.broadcasted_iota(jnp.int32, (CHUNK, PAIR), 1)
    col = lane & (RW_HEAD_DIM - 1)
    is_a = lane < RW_HEAD_DIM
    blockdiag = (row // INV_BLOCK) == (col // INV_BLOCK)
    eye = jnp.where(row == col, 1.0, 0.0).astype(F32)
    row_s = lax.broadcasted_iota(jnp.int32, (CHUNK, CHUNK), 0)
    col_s = lax.broadcasted_iota(jnp.int32, (CHUNK, CHUNK), 1)
    ltri = jnp.where(row_s >= col_s, 1.0, 0.0).astype(BF16)
    masks = (jnp.where(is_a, 1.0, 0.0).astype(BF16), jnp.where(is_a, 0.0, 1.0).astype(BF16))
    consts = (is_a, masks, row >= col, row > col, blockdiag, eye, ltri)

    chains = [(s, p) for s in range(r_ref.shape[0]) for p in range(pairs)]

    def body(c, carry):
        t0 = pl.multiple_of(c * CHUNK, CHUNK)
        gens = []
        for s, p in chains:
            sl = slice(p * PAIR, (p + 1) * PAIR)
            ld = lambda ref: ref[s, pl.ds(t0, CHUNK), sl]
            gens.append(_rwkv_chunk(ld(r_ref), ld(lw_ref), ld(k_ref), ld(v_ref), ld(a_ref),
                                    kk_ref[:, sl], ka_ref[:, sl], rk_ref[:, sl], gw_ref[:, sl], gb_ref[:, sl],
                                    state_ref[s * pairs + p], consts))
        results = [None] * len(chains)
        while any(res is None for res in results):
            for i, gen in enumerate(gens):
                if results[i] is None:
                    try:
                        next(gen)
                    except StopIteration as done:
                        results[i] = done.value
        for (s, p), (y, st) in zip(chains, results):
            state_ref[s * pairs + p] = st
            o_ref[s, pl.ds(t0, CHUNK), p * PAIR:(p + 1) * PAIR] = y
        return carry

    lax.fori_loop(0, n_chunks, body, 0)


def _rwkv_scan(r, lw, k, v, a, k_k, k_a, r_k, gn_w, gn_b, *, batch, seq, pairs=8, seqs=2, tt=256):
    T, D = r.shape
    tt = min(tt, seq)
    seqs = min(seqs, batch)
    lanes = pairs * PAIR
    seq_spec = pl.BlockSpec((seqs, tt, lanes), lambda b, p, t: (b, t, p))
    par_spec = pl.BlockSpec((1, lanes), lambda b, p, t: (0, p))
    as3 = lambda t: t.reshape(batch, seq, D)
    row = lambda t: t.reshape(1, D)
    y = pl.pallas_call(
        functools.partial(_rwkv_scan_kernel, n_chunks=tt // CHUNK, pairs=pairs),
        grid=(batch // seqs, D // lanes, seq // tt),
        in_specs=[seq_spec] * 5 + [par_spec] * 5,
        out_specs=seq_spec,
        out_shape=jax.ShapeDtypeStruct((batch, seq, D), F32),
        scratch_shapes=[pltpu.VMEM((seqs * pairs, RW_HEAD_DIM, PAIR), F32)],
        compiler_params=_cparams(("parallel", "parallel", "arbitrary")),
        name="rwkv_scan",
    )(as3(r), as3(lw), as3(k), as3(v), as3(a), row(k_k), row(k_a), row(r_k), row(gn_w), row(gn_b))
    return y.reshape(T, D)


def _group_rms(x, gain, pool, expand):
    pooled = _dot((x * x).astype(BF16), pool)
    inv = lax.rsqrt(pooled * (1.0 / HEAD_DIM) + NORM_EPS)
    hi = inv.astype(BF16)
    lo = (inv - hi.astype(F32)).astype(BF16)
    return x * (_dot(hi, expand) + _dot(lo, expand)) * gain


def _nsa_kv_kernel(x_ref, gn_ref, w_ref, kn_ref, cos_ref, sin_ref, pool_ref, exp_ref,
                   kc_ref, vc_ref, ks_ref, vs_ref, kw_ref, vw_ref):
    h = _rms(x_ref[...], gn_ref[...]).astype(BF16)
    kv = _dot(h, w_ref[...])
    W = NSA_GROUPS * HEAD_DIM
    part = lambda i: kv[:, i * W:(i + 1) * W]
    cos, sin = cos_ref[...], sin_ref[...]
    pool, expand = pool_ref[...], exp_ref[...]

    def put(ref, val):
        for g in range(NSA_GROUPS):
            ref[g] = val[:, g * HEAD_DIM:(g + 1) * HEAD_DIM].astype(ref.dtype)

    put(kc_ref, part(0))
    put(vc_ref, part(1))
    put(ks_ref, _rope(_group_rms(part(2), kn_ref[1:2, :], pool, expand), cos, sin))
    put(vs_ref, part(3))
    put(kw_ref, _rope(_group_rms(part(4), kn_ref[2:3, :], pool, expand), cos, sin))
    put(vw_ref, part(5))


def _pool_matrices(width):
    head = np.arange(width) // HEAD_DIM
    pool = (head[:, None] == np.arange(LANES)[None, :]).astype(np.float32)
    return jnp.asarray(pool, BF16), jnp.asarray(pool.T, BF16)


def _rope_tables(seq, width):
    half = HEAD_DIM // 2
    inv = ROPE_THETA ** (-jnp.arange(half, dtype=F32) / half)
    ang = jnp.arange(seq).astype(F32)[:, None] * inv[None, :]
    cos, sin = jnp.cos(ang), jnp.sin(ang)
    reps = width // HEAD_DIM
    return (jnp.tile(jnp.concatenate([cos, cos], axis=1), (1, reps)),
            jnp.tile(jnp.concatenate([-sin, sin], axis=1), (1, reps)))


def _nsa_kv(x, kv_norm, kv_w, kv_k_norm, *, batch, seq, tm=512):
    T, D = x.shape
    tm = min(tm, seq)
    G, dh = NSA_GROUPS, HEAD_DIM
    W = G * dh
    cos, sin = _rope_tables(seq, W)
    pool, expand = _pool_matrices(W)
    kn = jnp.tile(kv_k_norm, (1, G))
    tps = seq // tm
    full = lambda a: pl.BlockSpec(a.shape, lambda b, t: (0,) * a.ndim)
    tab = pl.BlockSpec((tm, W), lambda b, t: (t, 0))
    out_tile = pl.BlockSpec((None, G, tm, dh), lambda b, t: (b, 0, t, 0))
    w = kv_w.astype(BF16)
    shape = (batch, G, seq, dh)
    return pl.pallas_call(
        _nsa_kv_kernel,
        grid=(batch, tps),
        in_specs=[pl.BlockSpec((tm, D), lambda b, t: (b * tps + t, 0)), pl.BlockSpec((1, D), lambda b, t: (0, 0)),
                  full(w), full(kn), tab, tab, full(pool), full(expand)],
        out_specs=[out_tile] * 6,
        out_shape=[jax.ShapeDtypeStruct(shape, F32)] * 2 + [jax.ShapeDtypeStruct(shape, BF16)] * 4,
        compiler_params=_cparams(("parallel", "parallel")),
        name="nsa_kv",
    )(x, kv_norm.reshape(1, D), w, kn, cos, sin, pool, expand)


def _compress(tok_ref, pos_ref, w1_ref, w2_ref):
    nch = tok_ref.shape[0] // D_CMP
    dh = tok_ref.shape[1]
    lo = hi = None
    for l in range(D_CMP):
        rows = tok_ref[pl.ds(l, nch, stride=D_CMP), :]
        for half in range(L_CMP // D_CMP):
            p = half * D_CMP + l
            term = _dot((rows + pos_ref[p:p + 1, :]).astype(BF16), w1_ref[p * dh:(p + 1) * dh, :])
            if half == 0:
                lo = term if lo is None else lo + term
            else:
                hi = term if hi is None else hi + term
    hid = lo + pltpu.roll(hi, nch - 1, axis=0)
    act = 0.5 * hid * (1.0 + jnp.tanh(np.sqrt(2.0 / np.pi) * (hid + 0.044715 * hid * hid * hid)))
    return _dot(act.astype(BF16), w2_ref[...])


def _nsa_cmp_kernel(ck_ref, cv_ref, pk_ref, pv_ref, w1k_ref, w2k_ref, w1v_ref, w2v_ref, kn_ref,
                    ko_ref, vo_ref):
    ko_ref[...] = _rms(_compress(ck_ref, pk_ref, w1k_ref, w2k_ref), kn_ref[...])
    vo_ref[...] = _compress(cv_ref, pv_ref, w1v_ref, w2v_ref)


def _nsa_cmp(kc_tok, vc_tok, kn0, pos_k, w1k, w2k, pos_v, w1v, w2v, *, batch, seq):
    G, dh = NSA_GROUPS, HEAD_DIM
    assert L_CMP == 2 * D_CMP
    nch = seq // D_CMP
    full = lambda a: pl.BlockSpec(a.shape, lambda b, g: (0,) * a.ndim)
    blk = pl.BlockSpec((None, None, seq, dh), lambda b, g: (b, g, 0, 0))
    oblk = pl.BlockSpec((None, nch, dh), lambda b, g: (b * G + g, 0, 0))
    args = [pos_k, pos_v, w1k.astype(BF16), w2k.astype(BF16), w1v.astype(BF16), w2v.astype(BF16),
            kn0.reshape(1, dh)]
    return pl.pallas_call(
        _nsa_cmp_kernel,
        grid=(batch, G),
        in_specs=[blk, blk] + [full(a) for a in args],
        out_specs=[oblk, oblk],
        out_shape=[jax.ShapeDtypeStruct((batch * G, nch, dh), F32)] * 2,
        compiler_params=_cparams(("parallel", "parallel")),
        name="nsa_cmp",
    )(kc_tok, vc_tok, *args)


def _nsa_q_kernel(x_ref, gn_ref, wq_ref, wg_ref, qn_ref, cos_ref, sin_ref, pool_ref, exp_ref,
                  q_ref, qr_ref, gate_ref):
    h = _rms(x_ref[...], gn_ref[...]).astype(BF16)
    q = _group_rms(_dot(h, wq_ref[...]), qn_ref[...], pool_ref[...], exp_ref[...])
    reps = q.shape[1] // cos_ref.shape[1]
    cos = jnp.concatenate([cos_ref[...]] * reps, axis=1)
    sin = jnp.concatenate([sin_ref[...]] * reps, axis=1)
    scale = LOGIT_SCALE
    q_ref[...] = (q * scale).astype(BF16)
    qr_ref[...] = (_rope(q, cos, sin) * scale).astype(BF16)
    gate_ref[...] = _sigmoid(_dot(h, wg_ref[...]))


def _nsa_q(x, gmix, w_q, q_norm, *, seq, tm=512):
    T, D = x.shape
    tm = min(tm, seq)
    HD = NSA_HEADS * HEAD_DIM
    cos, sin = _rope_tables(seq, LANES)
    pool, expand = _pool_matrices(HD)
    wq = w_q[:, :HD].astype(BF16)
    wg = jnp.pad(w_q[:, HD:], ((0, 0), (0, LANES - 3 * NSA_HEADS))).astype(BF16)
    qn = jnp.tile(q_norm.reshape(1, HEAD_DIM), (1, NSA_HEADS))
    tps = seq // tm
    full = lambda a: pl.BlockSpec(a.shape, lambda i: (0,) * a.ndim)
    tab = pl.BlockSpec((tm, LANES), lambda i: (i % tps, 0))
    tile = lambda w: pl.BlockSpec((tm, w), lambda i: (i, 0))
    return pl.pallas_call(
        _nsa_q_kernel,
        grid=(T // tm,),
        in_specs=[tile(D), pl.BlockSpec((1, D), lambda i: (0, 0)), full(wq), full(wg), full(qn),
                  tab, tab, full(pool), full(expand)],
        out_specs=[tile(HD), tile(HD), tile(LANES)],
        out_shape=[jax.ShapeDtypeStruct((T, HD), BF16)] * 2 + [jax.ShapeDtypeStruct((T, LANES), F32)],
        compiler_params=_cparams(("parallel",)),
        name="nsa_q",
    )(x, gmix.reshape(1, D), wq, wg, qn, cos, sin, pool, expand)


RANK_ROWS = SUBLANES
MAX_SAFE_SHIFT = 43.0


def _topk_bias(score, n_top):
    nsel = score.shape[0]
    groups = [score[lo:lo + RANK_ROWS, :] for lo in range(0, nsel, RANK_ROWS)]
    ranks = [jnp.zeros(g.shape, jnp.int32) for g in groups]
    for i in range(nsel):
        si = score[i:i + 1, :]
        for gi, sg in enumerate(groups):
            lo = gi * RANK_ROWS
            if lo > i:
                beats = si >= sg
            elif lo + RANK_ROWS - 1 < i:
                beats = si > sg
            else:
                later = lax.broadcasted_iota(jnp.int32, sg.shape, 0) + lo > i
                beats = (si > sg) | ((si == sg) & later)
            ranks[gi] = ranks[gi] + jnp.where(beats, 1, 0)
    return [jnp.where(r < n_top, 0.0, NEG_INF) for r in ranks]


def _nsa_attn_kernel(*refs):
    bound_ref = refs[11]
    safe = ((bound_ref[0] <= MAX_SAFE_SHIFT) & (bound_ref[1] <= MAX_SAFE_SHIFT)
            & (bound_ref[2] <= MAX_SAFE_SHIFT))
    pl.when(safe)(functools.partial(_nsa_attn_step, *refs, use_bound=True))
    pl.when(jnp.logical_not(safe))(functools.partial(_nsa_attn_step, *refs, use_bound=False))


def _nsa_attn_step(q_ref, qr_ref, gate_ref, kc_ref, vc_ref, ks_ref, vs_ref, kw_ref, vw_ref, msel_ref,
                   wbias_ref, bound_ref, o_ref, bias_ref, *, use_bound):
    TQ, HPG, dh = Q_TILE, HEADS_PER_GROUP, HEAD_DIM
    NQ = HPG * TQ
    qi = pl.program_id(2)
    s0 = qi * TQ
    heads = lambda ref: [ref[:, hh * dh:(hh + 1) * dh] for hh in range(HPG)]
    q_heads, qr_heads = heads(q_ref), heads(qr_ref)
    qk = lambda keys, qs: jnp.concatenate([_dot_nt(keys, qh) for qh in qs], axis=1)
    over_heads = lambda a: jnp.concatenate([a] * HPG, axis=1)
    rows = lambda shape: lax.broadcasted_iota(jnp.int32, shape, 0)
    cols = lambda shape: lax.broadcasted_iota(jnp.int32, shape, 1)

    def shifted(masked_logits, bound):
        if use_bound:
            return masked_logits(bound)
        s = masked_logits(0.0)
        return s - jnp.max(s, axis=0, keepdims=True)

    def softmax_then(s, finish):
        e = jnp.exp2(s)
        return finish(e, jnp.sum(e, axis=0, keepdims=True))

    ncp = kc_ref.shape[0]
    shape = (ncp, TQ)
    cmp_bias = jnp.where(rows(shape) * D_CMP + (L_CMP - 1) <= s0 + cols(shape), 0.0, NEG_INF)
    sees_any = s0 + (cols((1, NQ)) & (TQ - 1)) >= L_CMP - 1

    def cmp_finish(e, l):
        p = e * jnp.where(sees_any, 1.0 / l, 0.0)
        imp = p[:, 0:TQ]
        for hh in range(1, HPG):
            imp = imp + p[:, hh * TQ:(hh + 1) * TQ]
        return _dot_tn(vc_ref[...].astype(BF16), p.astype(BF16)), imp

    s_cmp = shifted(lambda shift: qk(kc_ref[...].astype(BF16), q_heads) + over_heads(cmp_bias - shift),
                    bound_ref[0])
    o_cmp, imp = softmax_then(s_cmp, cmp_finish)

    span = WINDOW + TQ
    start = pl.multiple_of(jnp.maximum(s0 - WINDOW, 0), TQ)
    s_win = shifted(lambda shift: qk(kw_ref[pl.ds(start, span), :], qr_heads) + over_heads(wbias_ref[...] - shift),
                    bound_ref[2])
    o_win = softmax_then(s_win, lambda e, l: _dot_tn(vw_ref[pl.ds(start, span), :], e.astype(BF16)) / l)

    p_slc = _sel_left(msel_ref[...], imp)
    nsel = p_slc.shape[0]
    shape = (nsel, TQ)
    blk = rows(shape)
    t = s0 + cols(shape)
    cur = t >> int(np.log2(L_SEL))
    forced = (blk == 0) | (blk == cur) | (blk == cur - 1)
    score = jnp.where(forced, FORCE_SCORE, p_slc)
    score = jnp.where(blk * L_SEL > t, -1.0, score)
    for gi, group_bias in enumerate(_topk_bias(score, min(N_SEL, nsel))):
        bias_ref[gi] = group_bias

    KT = RANK_ROWS * L_SEL
    last = (qi * TQ) // KT
    shape = (KT, TQ)
    causal_bias = jnp.where(rows(shape) <= cols(shape) + (s0 - last * KT), 0.0, NEG_INF)

    def tile_logits(j):
        return qk(ks_ref[pl.ds(pl.multiple_of(j * KT, KT), KT), :], qr_heads)

    def sel_slabs(logits, j, diagonal, shift):
        bias = bias_ref[j]
        for b in range(RANK_ROWS):
            blk_rows = slice(b * L_SEL, (b + 1) * L_SEL)
            mask = bias[b:b + 1, :] + causal_bias[blk_rows, :] if diagonal else bias[b:b + 1, :]
            yield logits[blk_rows, :] + (over_heads(mask) - shift)

    def sel_max(j, m, diagonal):
        for s in sel_slabs(tile_logits(j), j, diagonal, 0.0):
            m = jnp.maximum(m, jnp.max(s, axis=0, keepdims=True))
        return m

    def true_max():
        m = lax.fori_loop(0, last, lambda j, m: sel_max(j, m, False), jnp.full((1, NQ), NEG_INF, F32))
        return sel_max(last, m, True)

    m_sel = jnp.full((1, NQ), bound_ref[1], F32) if use_bound else true_max()

    def sel_accumulate(tiles, carry, diagonal):
        l, acc = carry
        for j, logits in [(j, tile_logits(j)) for j in tiles]:
            ps = []
            for s in sel_slabs(logits, j, diagonal, m_sel):
                p = jnp.exp2(s)
                l = l + jnp.sum(p.reshape(L_SEL // SUBLANES, SUBLANES, NQ), axis=0)
                ps.append(p.astype(BF16))
            acc = acc + _dot_tn(vs_ref[pl.ds(pl.multiple_of(j * KT, KT), KT), :], jnp.concatenate(ps, axis=0))
        return l, acc

    carry = (jnp.zeros((SUBLANES, NQ), F32), jnp.zeros((dh, NQ), F32))
    carry = lax.fori_loop(0, last // 2, lambda i, c: sel_accumulate([2 * i, 2 * i + 1], c, False), carry)
    carry = lax.fori_loop(2 * (last // 2), last, lambda j, c: sel_accumulate([j], c, False), carry)
    l_sel, acc_sel = sel_accumulate([last], carry, True)
    o_sel = acc_sel / jnp.sum(l_sel, axis=0, keepdims=True)

    outs = []
    for hh in range(HPG):
        sl = slice(hh * TQ, (hh + 1) * TQ)
        g = lambda c: gate_ref[3 * hh + c:3 * hh + c + 1, :]
        o_h = g(0) * o_cmp[:, sl] + g(1) * o_sel[:, sl] + g(2) * o_win[:, sl]
        outs.append(o_h.T)
    o_ref[...] = jnp.concatenate(outs, axis=1)


def _selection_matrix(seq):
    nsel, ncp = seq // L_SEL, seq // D_CMP
    stride = L_SEL // D_CMP
    lpad = L_CMP // D_CMP - 1
    n = np.arange(ncp)[None, :]
    j = np.arange(nsel)[:, None]
    m = (n >= stride * j - lpad) & (n <= stride * j + stride - 1)
    return jnp.asarray(m.astype(np.float32), BF16)


def _window_bias():
    r = np.arange(WINDOW + Q_TILE)[:, None]
    c = np.arange(Q_TILE)[None, :]
    offs = [min(i * Q_TILE, WINDOW) for i in range(WINDOW // Q_TILE + 1)]
    masks = [np.where((r <= c + off) & (r > c + off - WINDOW), 0.0, NEG_INF) for off in offs]
    return jnp.asarray(np.stack(masks), F32)


def _nsa_attn(q, qr, gates, bounds, k_cmp, v_cmp, k_sel, v_sel, k_win, v_win, *, batch, seq):
    G, HPG, dh = NSA_GROUPS, HEADS_PER_GROUP, HEAD_DIM
    T = batch * seq
    nt = seq // Q_TILE
    gw = G * HPG * 3
    gt = jnp.transpose(gates[:, :gw].reshape(batch, seq, G, HPG * 3), (0, 2, 3, 1))
    gt = jnp.pad(gt, ((0, 0), (0, 0), (0, GATE_ROWS - HPG * 3), (0, 0)))
    ncp = seq // D_CMP
    msel = _selection_matrix(seq)
    wbias = _window_bias()
    key_tile = RANK_ROWS * L_SEL
    assert seq % key_tile == 0 and key_tile % Q_TILE == 0
    qspec = pl.BlockSpec((Q_TILE, HPG * dh), lambda b, g, i: (b * nt + i, g))
    cspec = pl.BlockSpec((None, ncp, dh), lambda b, g, i: (b * G + g, 0, 0))
    kspec = pl.BlockSpec((None, None, seq, dh), lambda b, g, i: (b, g, 0, 0))
    return pl.pallas_call(
        _nsa_attn_kernel,
        grid=(batch, G, nt),
        in_specs=[qspec, qspec, pl.BlockSpec((None, None, GATE_ROWS, Q_TILE), lambda b, g, i: (b, g, 0, i)),
                  cspec, cspec, kspec, kspec, kspec, kspec,
                  pl.BlockSpec(msel.shape, lambda b, g, i: (0, 0)),
                  pl.BlockSpec((None,) + wbias.shape[1:],
                               lambda b, g, i: (jnp.minimum(i, wbias.shape[0] - 1), 0, 0)),
                  pl.BlockSpec(memory_space=pltpu.SMEM)],
        out_specs=qspec,
        out_shape=jax.ShapeDtypeStruct((T, G * HPG * dh), F32),
        scratch_shapes=[pltpu.VMEM((seq // key_tile, RANK_ROWS, Q_TILE), F32)],
        compiler_params=_cparams(("parallel", "parallel", "arbitrary")),
        name="nsa_attn",
    )(q, qr, gt, k_cmp, v_cmp, k_sel, v_sel, k_win, v_win, msel, wbias, bounds)


def _logit_bound(q_gain, k_gain):
    rounding = 1.02
    return (HEAD_DIM * LOGIT_SCALE * rounding) * jnp.max(jnp.abs(q_gain)) * jnp.max(jnp.abs(k_gain))


def kernel(x, norm_ffn1, ffn1_w_gate, ffn1_w_up, ffn1_w_down, norm_mix, norm_ffn2, ffn2_w_gate, ffn2_w_up, ffn2_w_down, rw_mu, rw_w_rkv, rw_w0, rw_w_lora_a, rw_w_lora_b, rw_a0, rw_a_lora_a, rw_a_lora_b, rw_g_lora_a, rw_g_lora_b, rw_k_k, rw_k_a, rw_r_k, rw_gn_w, rw_gn_b, rw_w_o, kv_norm, kv_w, kv_k_norm, cmp_pos_k, cmp_k_w1, cmp_k_w2, cmp_pos_v, cmp_v_w1, cmp_v_w2, nsa_w_q, nsa_q_norm, nsa_w_o):
    B, S, D = x.shape
    depth = norm_ffn1.shape[0]
    n_rwkv = rw_mu.shape[0]
    assert S % max(Q_TILE, CHUNK) == 0 and S >= WINDOW + Q_TILE and D % LANES == 0
    xt = x.reshape(B * S, D)
    shared = None
    for l in range(depth):
        xt = _ffn(xt, norm_ffn1[l], ffn1_w_gate[l], ffn1_w_up[l], ffn1_w_down[l])
        if l < n_rwkv:
            i = l
            r, lw, k, v, a, g = _rwkv_prep(
                xt, norm_mix[l], rw_mu[i], rw_w_rkv[i], rw_w0[i], rw_w_lora_a[i], rw_w_lora_b[i], rw_a0[i],
                rw_a_lora_a[i], rw_a_lora_b[i], rw_g_lora_a[i], rw_g_lora_b[i], seq=S)
            y = _rwkv_scan(r, lw, k, v, a, rw_k_k[i], rw_k_a[i], rw_r_k[i], rw_gn_w[i], rw_gn_b[i],
                           batch=B, seq=S)
            mixer = (y, g, rw_w_o[i])
        else:
            i = l - n_rwkv
            q, qr, gates = _nsa_q(xt, norm_mix[l], nsa_w_q[i], nsa_q_norm[i], seq=S)
            bounds = jnp.stack([_logit_bound(nsa_q_norm[i], kv_k_norm[br]) for br in range(3)])
            o = _nsa_attn(q, qr, gates, bounds, *shared, batch=B, seq=S)
            mixer = (o, None, nsa_w_o[i])
        xt = _ffn(xt, norm_ffn2[l], ffn2_w_gate[l], ffn2_w_up[l], ffn2_w_down[l], mixer=mixer)
        if l == n_rwkv - 1:
            kc_tok, vc_tok, k_sel, v_sel, k_win, v_win = _nsa_kv(xt, kv_norm, kv_w, kv_k_norm, batch=B, seq=S)
            k_cmp, v_cmp = _nsa_cmp(kc_tok, vc_tok, kv_k_norm[0], cmp_pos_k, cmp_k_w1, cmp_k_w2,
                                    cmp_pos_v, cmp_v_w1, cmp_v_w2, batch=B, seq=S)
            shared = (k_cmp, v_cmp, k_sel, v_sel, k_win, v_win)
    return xt.reshape(B, S, D)
```

```python
import functools

import numpy as np
import jax
import jax.numpy as jnp
from jax import lax
from jax.experimental import pallas as pl
from jax.experimental.pallas import tpu as pltpu

F32 = jnp.float32
BF16 = jnp.bfloat16

NORM_EPS = 1e-6
RW_HEAD_DIM = 64
RW_GN_EPS = 64e-5
NSA_HEADS = 16
NSA_GROUPS = 4
HEADS_PER_GROUP = NSA_HEADS // NSA_GROUPS
HEAD_DIM = 64
L_CMP = 32
D_CMP = 16
L_SEL = 64
N_SEL = 16
WINDOW = 512
ROPE_THETA = 10000.0
FORCE_SCORE = 1e6
NEG_INF = -1e30
LOGIT_SCALE = HEAD_DIM ** -0.5 * float(np.log2(np.e))

LANES = 128
SUBLANES = 8
PACKED_SUBLANES = 16
VMEM_LIMIT = 56 * 1024 * 1024

CHUNK = 64
INV_BLOCK = 16
Q_TILE = 256
GATE_ROWS = PACKED_SUBLANES


def _cparams(sem):
    return pltpu.CompilerParams(dimension_semantics=sem, vmem_limit_bytes=VMEM_LIMIT)


def _dot(a, b):
    return jnp.dot(a, b, preferred_element_type=F32)


def _dot_nt(a, b):
    return lax.dot_general(a, b, (((1,), (1,)), ((), ())), preferred_element_type=F32)


def _dot_tn(a, b):
    return lax.dot_general(a, b, (((0,), (0,)), ((), ())), preferred_element_type=F32)


def _split3(x):
    hi = x.astype(BF16)
    r1 = x - hi.astype(F32)
    mid = r1.astype(BF16)
    lo = (r1 - mid.astype(F32)).astype(BF16)
    return hi, mid, lo


def _sel_left(m01, x):
    x1, x2, x3 = _split3(x)
    return _dot(m01, x1) + (_dot(m01, x2) + _dot(m01, x3))


def _rms(x, g):
    return x * lax.rsqrt(jnp.mean(x * x, axis=-1, keepdims=True) + NORM_EPS) * g


def _sigmoid(z):
    return 1.0 / (1.0 + jnp.exp(-z))


def _rope(x, cos, sin_signed):
    n = x.shape[-1]
    half = HEAD_DIM // 2
    lane = lax.broadcasted_iota(jnp.int32, x.shape, 1)
    first = (lane & (HEAD_DIM - 1)) < half
    partner = jnp.where(first, pltpu.roll(x, n - half, axis=1), pltpu.roll(x, half, axis=1))
    return x * cos + partner * sin_signed


def _ffn_kernel(*refs, mixer, gated):
    refs = list(refs)
    x = refs.pop(0)[...]
    if mixer:
        y = refs.pop(0)[...]
        if gated:
            y = y * refs.pop(0)[...]
        x = x + _dot(y.astype(BF16), refs.pop(0)[...])
    g_ref, wg_ref, wu_ref, wd_ref, o_ref = refs
    h = _rms(x, g_ref[...]).astype(BF16)
    gate = _dot(h, wg_ref[...])
    up = _dot(h, wu_ref[...])
    act = (gate * _sigmoid(gate) * up).astype(BF16)
    o_ref[...] = x + 0.5 * _dot(act, wd_ref[...])


def _resident(shape):
    return pl.BlockSpec(shape, lambda *_: (0,) * len(shape), pipeline_mode=pl.Buffered(1))


def _resident_layer(stacked, layer):
    return pl.BlockSpec((None,) + stacked.shape[1:], lambda *_: (layer,) + (0,) * (stacked.ndim - 1),
                        pipeline_mode=pl.Buffered(1))


def _ffn(x, g, wg, wu, wd, layer, *, mixer=None, tm=512):
    T, D = x.shape
    tm = min(tm, T)
    tile = pl.BlockSpec((tm, D), lambda i: (i, 0))
    ins, specs = [x], [tile]
    if mixer is not None:
        y, y_gate, w_o = mixer
        ins += [y] + ([y_gate] if y_gate is not None else []) + [w_o.astype(BF16)]
        specs += [tile] * (len(ins) - 2) + [_resident(w_o.shape)]
    ins += [g.reshape(1, D), wg.astype(BF16), wu.astype(BF16), wd.astype(BF16)]
    specs += [_resident((1, D)), _resident_layer(wg, layer), _resident_layer(wu, layer), _resident_layer(wd, layer)]
    return pl.pallas_call(
        functools.partial(_ffn_kernel, mixer=mixer is not None, gated=mixer is not None and mixer[1] is not None),
        grid=(T // tm,),
        in_specs=specs,
        out_specs=tile,
        out_shape=jax.ShapeDtypeStruct((T, D), F32),
        compiler_params=_cparams(("parallel",)),
        name="ffn",
    )(*ins)


def _rwkv_prep_kernel(x_ref, xp_ref, gm_ref, mu_ref, wrkv_ref, w0_ref, wla_ref, wlb_ref, a0_ref,
                      ala_ref, alb_ref, gla_ref, glb_ref,
                      r_ref, lw_ref, k_ref, v_ref, a_ref, g_ref, *, tiles_per_seq):
    i = pl.program_id(0)
    gm = gm_ref[...]
    h = _rms(x_ref[...], gm)
    hp = _rms(xp_ref[...], gm)[SUBLANES - 1:SUBLANES, :]
    hp = jnp.where(i % tiles_per_seq == 0, 0.0, hp)
    row = lax.broadcasted_iota(jnp.int32, h.shape, 0)
    prev = jnp.where(row == 0, hp, pltpu.roll(h, 1, axis=0))
    xx = prev - h
    mu = mu_ref[...]
    mix = lambda c: (h + xx * mu[c:c + 1, :]).astype(BF16)
    r_ref[...] = _dot(mix(0), wrkv_ref[0])
    k_ref[...] = _dot(mix(2), wrkv_ref[1])
    v_ref[...] = _dot(mix(3), wrkv_ref[2])
    z = w0_ref[...] + _dot(jnp.tanh(_dot(mix(1), wla_ref[...])).astype(BF16), wlb_ref[...])
    lw_ref[...] = -float(np.exp(-0.5) * np.log2(np.e)) * _sigmoid(z)
    a_ref[...] = _sigmoid(a0_ref[...] + _dot(_dot(mix(4), ala_ref[...]).astype(BF16), alb_ref[...]))
    g_ref[...] = _dot(_sigmoid(_dot(mix(5), gla_ref[...])).astype(BF16), glb_ref[...])


def _rwkv_prep(x, gmix, mu, w_rkv, w0, wla, wlb, a0, ala, alb, gla, glb, *, seq, tm=512):
    T, D = x.shape
    tm = min(tm, seq)
    full = lambda a: _resident(a.shape)
    row = lambda a: a.reshape(1, D)
    args = [gmix.reshape(1, D), mu, w_rkv.astype(BF16), row(w0), wla.astype(BF16), wlb.astype(BF16),
            row(a0), ala.astype(BF16), alb.astype(BF16), gla.astype(BF16), glb.astype(BF16)]
    out = jax.ShapeDtypeStruct((T, D), F32)
    return pl.pallas_call(
        functools.partial(_rwkv_prep_kernel, tiles_per_seq=seq // tm),
        grid=(T // tm,),
        in_specs=[pl.BlockSpec((tm, D), lambda i: (i, 0)),
                  pl.BlockSpec((SUBLANES, D), lambda i: (jnp.maximum(i * (tm // SUBLANES) - 1, 0), 0))]
                 + [full(a) for a in args],
        out_specs=[pl.BlockSpec((tm, D), lambda i: (i, 0))] * 6,
        out_shape=[out] * 6,
        compiler_params=_cparams(("parallel",)),
        name="rwkv_prep",
    )(x, x, *args)


PAIR = 2 * RW_HEAD_DIM


def _blockdiag(x, masks):
    return jnp.concatenate([x * masks[0], x * masks[1]], axis=0)


def _pair_dot(lhs, rhs_list, masks, transpose_rhs=False):
    dot = _dot_nt if transpose_rhs else _dot
    rhs = jnp.concatenate([_blockdiag(r.astype(BF16), masks) for r in rhs_list], axis=0 if transpose_rhs else 1)
    out = dot(lhs.astype(BF16), rhs)
    return [out[:, i * PAIR:(i + 1) * PAIR] for i in range(len(rhs_list))]


def _seg_sum(x, is_a):
    zero = jnp.zeros_like(x)
    sa = jnp.sum(jnp.where(is_a, x, zero), axis=-1, keepdims=True)
    sb = jnp.sum(jnp.where(is_a, zero, x), axis=-1, keepdims=True)
    return jnp.where(is_a, sa, sb)


def _rwkv_chunk(r, lw, k, v, a, kk_p, ka_p, rk_p, gn_w, gn_b, state, consts):
    is_a, masks, tri_incl, tri_strict, blockdiag, eye, ltri = consts
    inv_n = 1.0 / RW_HEAD_DIM
    kkr = k * kk_p
    kk = kkr * jnp.minimum(lax.rsqrt(_seg_sum(kkr * kkr, is_a)), 1e12)
    k2 = k * (1.0 + (a - 1.0) * ka_p)
    kb = kk * a

    cum = _sel_left(ltri, lw)
    yield
    cum_last = cum[CHUNK - 1:CHUNK, :]
    einv = jnp.exp2(-cum)
    elast = jnp.exp2(cum_last - cum)
    a_t = -kk * jnp.exp2(cum - lw)
    r_t = r * jnp.exp2(cum)
    b_h = kb * einv
    k_h = k2 * einv
    b_l = kb * elast
    k_l = k2 * elast

    ar = jnp.concatenate([a_t, r_t], axis=0)
    d1 = lambda p, qs: _pair_dot(p, qs, masks)
    sc_b, sc_k = _pair_dot(ar, [b_h, k_h], masks, transpose_rhs=True)
    yield
    m_ab = jnp.where(tri_strict, sc_b[:CHUNK], 0.0)
    m_rb = jnp.where(tri_incl, sc_b[CHUNK:], 0.0)
    m_ak = jnp.where(tri_strict, sc_k[:CHUNK], 0.0)
    m_rk = jnp.where(tri_incl, sc_k[CHUNK:], 0.0)

    (w1,) = d1(m_ak, [v])
    m_d = jnp.where(blockdiag, m_ab, 0.0)
    m_o = m_ab - m_d
    dinv = eye + m_d
    (pw,) = d1(m_d, [m_d])
    yield
    for _ in range(int(np.log2(INV_BLOCK)) - 2):
        (both,) = d1(jnp.concatenate([pw, dinv], axis=0), [pw])
        pw, dinv = both[:CHUNK], dinv + both[CHUNK:]
        yield
    dinv = dinv + d1(dinv, [pw])[0]
    yield
    n1, x1, x2 = d1(dinv, [m_o, a_t, w1])
    yield
    n2, u1, u2 = d1(n1, [n1, x1, x2])
    x1, x2 = x1 + u1, x2 + u2
    yield
    u1, u2 = d1(n2, [x1, x2])
    x1, x2 = x1 + u1, x2 + u2
    yield

    x1b, x2b, vb = x1.astype(BF16), x2.astype(BF16), v.astype(BF16)
    bd_v = _blockdiag(vb, masks)
    zy = _dot(jnp.concatenate([m_rb, m_rk], axis=1).astype(BF16),
              jnp.concatenate([jnp.concatenate([_blockdiag(x1b, masks), _blockdiag(x2b, masks)], axis=1),
                               jnp.concatenate([jnp.zeros_like(bd_v), bd_v], axis=1)], axis=0))
    g1 = r_t + zy[:, :PAIR]
    y_c = zy[:, PAIR:]
    z2 = _dot_tn(jnp.concatenate([b_l, k_l], axis=0).astype(BF16),
                 jnp.concatenate([jnp.concatenate([x1b, x2b], axis=1),
                                  jnp.concatenate([jnp.zeros_like(vb), vb], axis=1)], axis=0))
    yield
    diag_blocks = lambda f: jnp.where(is_a, f[:RW_HEAD_DIM], f[RW_HEAD_DIM:])
    trans = jnp.where(eye > 0, jnp.exp2(cum_last), 0.0) + diag_blocks(z2[:, :PAIR])
    add = diag_blocks(z2[:, PAIR:])
    (ys,) = d1(jnp.concatenate([g1, trans], axis=0), [state])
    yield
    y = ys[:CHUNK] + y_c
    new_state = ys[CHUNK:] + add

    mean = _seg_sum(y, is_a) * inv_n
    dev = y - mean
    var = _seg_sum(dev * dev, is_a) * inv_n
    yn = dev * lax.rsqrt(var + RW_GN_EPS) * gn_w + gn_b
    bonus = _seg_sum(r * k2 * rk_p, is_a) * v
    return yn + bonus, new_state


def _rwkv_scan_kernel(r_ref, lw_ref, k_ref, v_ref, a_ref, kk_ref, ka_ref, rk_ref, gw_ref, gb_ref, o_ref,
                      state_ref, *, n_chunks, pairs):
    @pl.when(pl.program_id(2) == 0)
    def _():
        state_ref[...] = jnp.zeros_like(state_ref)

    row = lax.broadcasted_iota(jnp.int32, (CHUNK, PAIR), 0)
    lane = lax.broadcasted_iota(jnp.int32, (CHUNK, PAIR), 1)
    col = lane & (RW_HEAD_DIM - 1)
    is_a = lane < RW_HEAD_DIM
    blockdiag = (row // INV_BLOCK) == (col // INV_BLOCK)
    eye = jnp.where(row == col, 1.0, 0.0).astype(F32)
    row_s = lax.broadcasted_iota(jnp.int32, (CHUNK, CHUNK), 0)
    col_s = lax.broadcasted_iota(jnp.int32, (CHUNK, CHUNK), 1)
    ltri = jnp.where(row_s >= col_s, 1.0, 0.0).astype(BF16)
    masks = (jnp.where(is_a, 1.0, 0.0).astype(BF16), jnp.where(is_a, 0.0, 1.0).astype(BF16))
    consts = (is_a, masks, row >= col, row > col, blockdiag, eye, ltri)

    chains = [(s, p) for s in range(r_ref.shape[0]) for p in range(pairs)]

    def body(c, carry):
        t0 = pl.multiple_of(c * CHUNK, CHUNK)
        gens = []
        for s, p in chains:
            sl = slice(p * PAIR, (p + 1) * PAIR)
            ld = lambda ref: ref[s, pl.ds(t0, CHUNK), sl]
            gens.append(_rwkv_chunk(ld(r_ref), ld(lw_ref), ld(k_ref), ld(v_ref), ld(a_ref),
                                    kk_ref[:, sl], ka_ref[:, sl], rk_ref[:, sl], gw_ref[:, sl], gb_ref[:, sl],
                                    state_ref[s * pairs + p], consts))
        results = [None] * len(chains)
        while any(res is None for res in results):
            for i, gen in enumerate(gens):
                if results[i] is None:
                    try:
                        next(gen)
                    except StopIteration as done:
                        results[i] = done.value
        for (s, p), (y, st) in zip(chains, results):
            state_ref[s * pairs + p] = st
            o_ref[s, pl.ds(t0, CHUNK), p * PAIR:(p + 1) * PAIR] = y
        return carry

    lax.fori_loop(0, n_chunks, body, 0)


def _rwkv_scan(r, lw, k, v, a, k_k, k_a, r_k, gn_w, gn_b, *, batch, seq, pairs=8, seqs=2, tt=256):
    T, D = r.shape
    tt = min(tt, seq)
    seqs = min(seqs, batch)
    lanes = pairs * PAIR
    seq_spec = pl.BlockSpec((seqs, tt, lanes), lambda b, p, t: (b, t, p))
    par_spec = pl.BlockSpec((1, lanes), lambda b, p, t: (0, p))
    as3 = lambda t: t.reshape(batch, seq, D)
    row = lambda t: t.reshape(1, D)
    y = pl.pallas_call(
        functools.partial(_rwkv_scan_kernel, n_chunks=tt // CHUNK, pairs=pairs),
        grid=(batch // seqs, D // lanes, seq // tt),
        in_specs=[seq_spec] * 5 + [par_spec] * 5,
        out_specs=seq_spec,
        out_shape=jax.ShapeDtypeStruct((batch, seq, D), F32),
        scratch_shapes=[pltpu.VMEM((seqs * pairs, RW_HEAD_DIM, PAIR), F32)],
        compiler_params=_cparams(("parallel", "parallel", "arbitrary")),
        name="rwkv_scan",
    )(as3(r), as3(lw), as3(k), as3(v), as3(a), row(k_k), row(k_a), row(r_k), row(gn_w), row(gn_b))
    return y.reshape(T, D)


def _group_rms(x, gain, pool, expand):
    pooled = _dot((x * x).astype(BF16), pool)
    inv = lax.rsqrt(pooled * (1.0 / HEAD_DIM) + NORM_EPS)
    hi = inv.astype(BF16)
    lo = (inv - hi.astype(F32)).astype(BF16)
    return x * (_dot(hi, expand) + _dot(lo, expand)) * gain


def _nsa_kv_kernel(x_ref, gn_ref, w_ref, kn_ref, cos_ref, sin_ref, pool_ref, exp_ref,
                   kc_ref, vc_ref, ks_ref, vs_ref, kw_ref, vw_ref):
    h = _rms(x_ref[...], gn_ref[...]).astype(BF16)
    kv = _dot(h, w_ref[...])
    W = NSA_GROUPS * HEAD_DIM
    part = lambda i: kv[:, i * W:(i + 1) * W]
    cos, sin = cos_ref[...], sin_ref[...]
    pool, expand = pool_ref[...], exp_ref[...]

    def put(ref, val):
        for g in range(NSA_GROUPS):
            ref[g] = val[:, g * HEAD_DIM:(g + 1) * HEAD_DIM].astype(ref.dtype)

    put(kc_ref, part(0))
    put(vc_ref, part(1))
    put(ks_ref, _rope(_group_rms(part(2), kn_ref[1:2, :], pool, expand), cos, sin))
    put(vs_ref, part(3))
    put(kw_ref, _rope(_group_rms(part(4), kn_ref[2:3, :], pool, expand), cos, sin))
    put(vw_ref, part(5))


def _pool_matrices(width):
    head = np.arange(width) // HEAD_DIM
    pool = (head[:, None] == np.arange(LANES)[None, :]).astype(np.float32)
    return jnp.asarray(pool, BF16), jnp.asarray(pool.T, BF16)


def _rope_tables(seq, width):
    half = HEAD_DIM // 2
    inv = ROPE_THETA ** (-jnp.arange(half, dtype=F32) / half)
    ang = jnp.arange(seq).astype(F32)[:, None] * inv[None, :]
    cos, sin = jnp.cos(ang), jnp.sin(ang)
    reps = width // HEAD_DIM
    return (jnp.tile(jnp.concatenate([cos, cos], axis=1), (1, reps)),
            jnp.tile(jnp.concatenate([-sin, sin], axis=1), (1, reps)))


def _nsa_kv(x, kv_norm, kv_w, kv_k_norm, *, batch, seq, tm=512):
    T, D = x.shape
    tm = min(tm, seq)
    G, dh = NSA_GROUPS, HEAD_DIM
    W = G * dh
    cos, sin = _rope_tables(seq, W)
    pool, expand = _pool_matrices(W)
    kn = jnp.tile(kv_k_norm, (1, G))
    tps = seq // tm
    full = lambda a: pl.BlockSpec(a.shape, lambda b, t: (0,) * a.ndim)
    tab = pl.BlockSpec((tm, W), lambda b, t: (t, 0))
    out_tile = pl.BlockSpec((None, G, tm, dh), lambda b, t: (b, 0, t, 0))
    w = kv_w.astype(BF16)
    shape = (batch, G, seq, dh)
    return pl.pallas_call(
        _nsa_kv_kernel,
        grid=(batch, tps),
        in_specs=[pl.BlockSpec((tm, D), lambda b, t: (b * tps + t, 0)), pl.BlockSpec((1, D), lambda b, t: (0, 0)),
                  full(w), full(kn), tab, tab, full(pool), full(expand)],
        out_specs=[out_tile] * 6,
        out_shape=[jax.ShapeDtypeStruct(shape, F32)] * 2 + [jax.ShapeDtypeStruct(shape, BF16)] * 4,
        compiler_params=_cparams(("parallel", "parallel")),
        name="nsa_kv",
    )(x, kv_norm.reshape(1, D), w, kn, cos, sin, pool, expand)


def _compress(tok_ref, pos_ref, w1_ref, w2_ref):
    nch = tok_ref.shape[0] // D_CMP
    dh = tok_ref.shape[1]
    lo = hi = None
    for l in range(D_CMP):
        rows = tok_ref[pl.ds(l, nch, stride=D_CMP), :]
        for half in range(L_CMP // D_CMP):
            p = half * D_CMP + l
            term = _dot((rows + pos_ref[p:p + 1, :]).astype(BF16), w1_ref[p * dh:(p + 1) * dh, :])
            if half == 0:
                lo = term if lo is None else lo + term
            else:
                hi = term if hi is None else hi + term
    hid = lo + pltpu.roll(hi, nch - 1, axis=0)
    act = 0.5 * hid * (1.0 + jnp.tanh(np.sqrt(2.0 / np.pi) * (hid + 0.044715 * hid * hid * hid)))
    return _dot(act.astype(BF16), w2_ref[...])


def _nsa_cmp_kernel(ck_ref, cv_ref, pk_ref, pv_ref, w1k_ref, w2k_ref, w1v_ref, w2v_ref, kn_ref,
                    ko_ref, vo_ref):
    ko_ref[...] = _rms(_compress(ck_ref, pk_ref, w1k_ref, w2k_ref), kn_ref[...])
    vo_ref[...] = _compress(cv_ref, pv_ref, w1v_ref, w2v_ref)


def _nsa_cmp(kc_tok, vc_tok, kn0, pos_k, w1k, w2k, pos_v, w1v, w2v, *, batch, seq):
    G, dh = NSA_GROUPS, HEAD_DIM
    assert L_CMP == 2 * D_CMP
    nch = seq // D_CMP
    full = lambda a: pl.BlockSpec(a.shape, lambda b, g: (0,) * a.ndim)
    blk = pl.BlockSpec((None, None, seq, dh), lambda b, g: (b, g, 0, 0))
    oblk = pl.BlockSpec((None, nch, dh), lambda b, g: (b * G + g, 0, 0))
    args = [pos_k, pos_v, w1k.astype(BF16), w2k.astype(BF16), w1v.astype(BF16), w2v.astype(BF16),
            kn0.reshape(1, dh)]
    return pl.pallas_call(
        _nsa_cmp_kernel,
        grid=(batch, G),
        in_specs=[blk, blk] + [full(a) for a in args],
        out_specs=[oblk, oblk],
        out_shape=[jax.ShapeDtypeStruct((batch * G, nch, dh), F32)] * 2,
        compiler_params=_cparams(("parallel", "parallel")),
        name="nsa_cmp",
    )(kc_tok, vc_tok, *args)


def _nsa_q_kernel(x_ref, gn_ref, wq_ref, wg_ref, qn_ref, cos_ref, sin_ref, pool_ref, exp_ref,
                  q_ref, qr_ref, gate_ref):
    h = _rms(x_ref[...], gn_ref[...]).astype(BF16)
    q = _group_rms(_dot(h, wq_ref[...]), qn_ref[...], pool_ref[...], exp_ref[...])
    reps = q.shape[1] // cos_ref.shape[1]
    cos = jnp.concatenate([cos_ref[...]] * reps, axis=1)
    sin = jnp.concatenate([sin_ref[...]] * reps, axis=1)
    scale = LOGIT_SCALE
    q_ref[...] = (q * scale).astype(BF16)
    qr_ref[...] = (_rope(q, cos, sin) * scale).astype(BF16)
    gate_ref[...] = _sigmoid(_dot(h, wg_ref[...]))


def _nsa_q(x, gmix, w_q, q_norm, *, seq, tm=512):
    T, D = x.shape
    tm = min(tm, seq)
    HD = NSA_HEADS * HEAD_DIM
    cos, sin = _rope_tables(seq, LANES)
    pool, expand = _pool_matrices(HD)
    wq = w_q[:, :HD].astype(BF16)
    wg = jnp.pad(w_q[:, HD:], ((0, 0), (0, LANES - 3 * NSA_HEADS))).astype(BF16)
    qn = jnp.tile(q_norm.reshape(1, HEAD_DIM), (1, NSA_HEADS))
    tps = seq // tm
    full = lambda a: pl.BlockSpec(a.shape, lambda i: (0,) * a.ndim)
    tab = pl.BlockSpec((tm, LANES), lambda i: (i % tps, 0))
    tile = lambda w: pl.BlockSpec((tm, w), lambda i: (i, 0))
    return pl.pallas_call(
        _nsa_q_kernel,
        grid=(T // tm,),
        in_specs=[tile(D), pl.BlockSpec((1, D), lambda i: (0, 0)), full(wq), full(wg), full(qn),
                  tab, tab, full(pool), full(expand)],
        out_specs=[tile(HD), tile(HD), tile(LANES)],
        out_shape=[jax.ShapeDtypeStruct((T, HD), BF16)] * 2 + [jax.ShapeDtypeStruct((T, LANES), F32)],
        compiler_params=_cparams(("parallel",)),
        name="nsa_q",
    )(x, gmix.reshape(1, D), wq, wg, qn, cos, sin, pool, expand)


RANK_ROWS = SUBLANES
MAX_SAFE_SHIFT = 43.0


def _topk_bias(score, n_top):
    nsel = score.shape[0]
    groups = [score[lo:lo + RANK_ROWS, :] for lo in range(0, nsel, RANK_ROWS)]
    ranks = [jnp.zeros(g.shape, jnp.int32) for g in groups]
    for i in range(nsel):
        si = score[i:i + 1, :]
        for gi, sg in enumerate(groups):
            lo = gi * RANK_ROWS
            if lo > i:
                beats = si >= sg
            elif lo + RANK_ROWS - 1 < i:
                beats = si > sg
            else:
                later = lax.broadcasted_iota(jnp.int32, sg.shape, 0) + lo > i
                beats = (si > sg) | ((si == sg) & later)
            ranks[gi] = ranks[gi] + jnp.where(beats, 1, 0)
    return [jnp.where(r < n_top, 0.0, NEG_INF) for r in ranks]


def _nsa_attn_kernel(*refs):
    bound_ref = refs[11]
    safe = ((bound_ref[0] <= MAX_SAFE_SHIFT) & (bound_ref[1] <= MAX_SAFE_SHIFT)
            & (bound_ref[2] <= MAX_SAFE_SHIFT))
    pl.when(safe)(functools.partial(_nsa_attn_step, *refs, use_bound=True))
    pl.when(jnp.logical_not(safe))(functools.partial(_nsa_attn_step, *refs, use_bound=False))


def _nsa_attn_step(q_ref, qr_ref, gate_ref, kc_ref, vc_ref, ks_ref, vs_ref, kw_ref, vw_ref, msel_ref,
                   wbias_ref, bound_ref, o_ref, bias_ref, *, use_bound):
    TQ, HPG, dh = Q_TILE, HEADS_PER_GROUP, HEAD_DIM
    NQ = HPG * TQ
    qi = pl.program_id(2)
    s0 = qi * TQ
    heads = lambda ref: [ref[:, hh * dh:(hh + 1) * dh] for hh in range(HPG)]
    q_heads, qr_heads = heads(q_ref), heads(qr_ref)
    qk = lambda keys, qs: jnp.concatenate([_dot_nt(keys, qh) for qh in qs], axis=1)
    over_heads = lambda a: jnp.concatenate([a] * HPG, axis=1)
    rows = lambda shape: lax.broadcasted_iota(jnp.int32, shape, 0)
    cols = lambda shape: lax.broadcasted_iota(jnp.int32, shape, 1)

    def shifted(masked_logits, bound):
        if use_bound:
            return masked_logits(bound)
        s = masked_logits(0.0)
        return s - jnp.max(s, axis=0, keepdims=True)

    def softmax_then(s, finish):
        e = jnp.exp2(s)
        return finish(e, jnp.sum(e, axis=0, keepdims=True))

    ncp = kc_ref.shape[0]
    shape = (ncp, TQ)
    cmp_bias = jnp.where(rows(shape) * D_CMP + (L_CMP - 1) <= s0 + cols(shape), 0.0, NEG_INF)
    sees_any = s0 + (cols((1, NQ)) & (TQ - 1)) >= L_CMP - 1

    def cmp_finish(e, l):
        p = e * jnp.where(sees_any, 1.0 / l, 0.0)
        imp = p[:, 0:TQ]
        for hh in range(1, HPG):
            imp = imp + p[:, hh * TQ:(hh + 1) * TQ]
        return _dot_tn(vc_ref[...].astype(BF16), p.astype(BF16)), imp

    s_cmp = shifted(lambda shift: qk(kc_ref[...].astype(BF16), q_heads) + over_heads(cmp_bias - shift),
                    bound_ref[0])
    o_cmp, imp = softmax_then(s_cmp, cmp_finish)

    span = WINDOW + TQ
    start = pl.multiple_of(jnp.maximum(s0 - WINDOW, 0), TQ)
    s_win = shifted(lambda shift: qk(kw_ref[pl.ds(start, span), :], qr_heads) + over_heads(wbias_ref[...] - shift),
                    bound_ref[2])
    o_win = softmax_then(s_win, lambda e, l: _dot_tn(vw_ref[pl.ds(start, span), :], e.astype(BF16)) / l)

    p_slc = _sel_left(msel_ref[...], imp)
    nsel = p_slc.shape[0]
    shape = (nsel, TQ)
    blk = rows(shape)
    t = s0 + cols(shape)
    cur = t >> int(np.log2(L_SEL))
    forced = (blk == 0) | (blk == cur) | (blk == cur - 1)
    score = jnp.where(forced, FORCE_SCORE, p_slc)
    score = jnp.where(blk * L_SEL > t, -1.0, score)
    for gi, group_bias in enumerate(_topk_bias(score, min(N_SEL, nsel))):
        bias_ref[gi] = group_bias

    KT = RANK_ROWS * L_SEL
    last = (qi * TQ) // KT
    shape = (KT, TQ)
    causal_bias = jnp.where(rows(shape) <= cols(shape) + (s0 - last * KT), 0.0, NEG_INF)

    def tile_logits(j):
        return qk(ks_ref[pl.ds(pl.multiple_of(j * KT, KT), KT), :], qr_heads)

    def sel_slabs(logits, j, diagonal, shift):
        bias = bias_ref[j]
        for b in range(RANK_ROWS):
            blk_rows = slice(b * L_SEL, (b + 1) * L_SEL)
            mask = bias[b:b + 1, :] + causal_bias[blk_rows, :] if diagonal else bias[b:b + 1, :]
            yield logits[blk_rows, :] + (over_heads(mask) - shift)

    def sel_max(j, m, diagonal):
        for s in sel_slabs(tile_logits(j), j, diagonal, 0.0):
            m = jnp.maximum(m, jnp.max(s, axis=0, keepdims=True))
        return m

    def true_max():
        m = lax.fori_loop(0, last, lambda j, m: sel_max(j, m, False), jnp.full((1, NQ), NEG_INF, F32))
        return sel_max(last, m, True)

    m_sel = jnp.full((1, NQ), bound_ref[1], F32) if use_bound else true_max()

    def sel_accumulate(tiles, carry, diagonal):
        l, acc = carry
        for j, logits in [(j, tile_logits(j)) for j in tiles]:
            ps = []
            for s in sel_slabs(logits, j, diagonal, m_sel):
                p = jnp.exp2(s)
                l = l + jnp.sum(p.reshape(L_SEL // SUBLANES, SUBLANES, NQ), axis=0)
                ps.append(p.astype(BF16))
            acc = acc + _dot_tn(vs_ref[pl.ds(pl.multiple_of(j * KT, KT), KT), :], jnp.concatenate(ps, axis=0))
        return l, acc

    carry = (jnp.zeros((SUBLANES, NQ), F32), jnp.zeros((dh, NQ), F32))
    carry = lax.fori_loop(0, last // 2, lambda i, c: sel_accumulate([2 * i, 2 * i + 1], c, False), carry)
    carry = lax.fori_loop(2 * (last // 2), last, lambda j, c: sel_accumulate([j], c, False), carry)
    l_sel, acc_sel = sel_accumulate([last], carry, True)
    o_sel = acc_sel / jnp.sum(l_sel, axis=0, keepdims=True)

    outs = []
    for hh in range(HPG):
        sl = slice(hh * TQ, (hh + 1) * TQ)
        g = lambda c: gate_ref[3 * hh + c:3 * hh + c + 1, :]
        o_h = g(0) * o_cmp[:, sl] + g(1) * o_sel[:, sl] + g(2) * o_win[:, sl]
        outs.append(o_h.T)
    o_ref[...] = jnp.concatenate(outs, axis=1)


def _selection_matrix(seq):
    nsel, ncp = seq // L_SEL, seq // D_CMP
    stride = L_SEL // D_CMP
    lpad = L_CMP // D_CMP - 1
    n = np.arange(ncp)[None, :]
    j = np.arange(nsel)[:, None]
    m = (n >= stride * j - lpad) & (n <= stride * j + stride - 1)
    return jnp.asarray(m.astype(np.float32), BF16)


def _window_bias():
    r = np.arange(WINDOW + Q_TILE)[:, None]
    c = np.arange(Q_TILE)[None, :]
    offs = [min(i * Q_TILE, WINDOW) for i in range(WINDOW // Q_TILE + 1)]
    masks = [np.where((r <= c + off) & (r > c + off - WINDOW), 0.0, NEG_INF) for off in offs]
    return jnp.asarray(np.stack(masks), F32)


def _nsa_attn(q, qr, gates, bounds, k_cmp, v_cmp, k_sel, v_sel, k_win, v_win, *, batch, seq):
    G, HPG, dh = NSA_GROUPS, HEADS_PER_GROUP, HEAD_DIM
    T = batch * seq
    nt = seq // Q_TILE
    gw = G * HPG * 3
    gt = jnp.transpose(gates[:, :gw].reshape(batch, seq, G, HPG * 3), (0, 2, 3, 1))
    gt = jnp.pad(gt, ((0, 0), (0, 0), (0, GATE_ROWS - HPG * 3), (0, 0)))
    ncp = seq // D_CMP
    msel = _selection_matrix(seq)
    wbias = _window_bias()
    key_tile = RANK_ROWS * L_SEL
    assert seq % key_tile == 0 and key_tile % Q_TILE == 0
    qspec = pl.BlockSpec((Q_TILE, HPG * dh), lambda b, g, i: (b * nt + i, g))
    cspec = pl.BlockSpec((None, ncp, dh), lambda b, g, i: (b * G + g, 0, 0))
    kspec = pl.BlockSpec((None, None, seq, dh), lambda b, g, i: (b, g, 0, 0))
    return pl.pallas_call(
        _nsa_attn_kernel,
        grid=(batch, G, nt),
        in_specs=[qspec, qspec, pl.BlockSpec((None, None, GATE_ROWS, Q_TILE), lambda b, g, i: (b, g, 0, i)),
                  cspec, cspec, kspec, kspec, kspec, kspec,
                  pl.BlockSpec(msel.shape, lambda b, g, i: (0, 0)),
                  pl.BlockSpec((None,) + wbias.shape[1:],
                               lambda b, g, i: (jnp.minimum(i, wbias.shape[0] - 1), 0, 0)),
                  pl.BlockSpec(memory_space=pltpu.SMEM)],
        out_specs=qspec,
        out_shape=jax.ShapeDtypeStruct((T, G * HPG * dh), F32),
        scratch_shapes=[pltpu.VMEM((seq // key_tile, RANK_ROWS, Q_TILE), F32)],
        compiler_params=_cparams(("parallel", "parallel", "arbitrary")),
        name="nsa_attn",
    )(q, qr, gt, k_cmp, v_cmp, k_sel, v_sel, k_win, v_win, msel, wbias, bounds)


def _logit_bound(q_gain, k_gain):
    rounding = 1.02
    return (HEAD_DIM * LOGIT_SCALE * rounding) * jnp.max(jnp.abs(q_gain)) * jnp.max(jnp.abs(k_gain))


def kernel(x, norm_ffn1, ffn1_w_gate, ffn1_w_up, ffn1_w_down, norm_mix, norm_ffn2, ffn2_w_gate, ffn2_w_up, ffn2_w_down, rw_mu, rw_w_rkv, rw_w0, rw_w_lora_a, rw_w_lora_b, rw_a0, rw_a_lora_a, rw_a_lora_b, rw_g_lora_a, rw_g_lora_b, rw_k_k, rw_k_a, rw_r_k, rw_gn_w, rw_gn_b, rw_w_o, kv_norm, kv_w, kv_k_norm, cmp_pos_k, cmp_k_w1, cmp_k_w2, cmp_pos_v, cmp_v_w1, cmp_v_w2, nsa_w_q, nsa_q_norm, nsa_w_o):
    B, S, D = x.shape
    depth = norm_ffn1.shape[0]
    n_rwkv = rw_mu.shape[0]
    assert S % max(Q_TILE, CHUNK) == 0 and S >= WINDOW + Q_TILE and D % LANES == 0
    xt = x.reshape(B * S, D)
    shared = None
    for l in range(depth):
        xt = _ffn(xt, norm_ffn1[l], ffn1_w_gate, ffn1_w_up, ffn1_w_down, l)
        if l < n_rwkv:
            i = l
            r, lw, k, v, a, g = _rwkv_prep(
                xt, norm_mix[l], rw_mu[i], rw_w_rkv[i], rw_w0[i], rw_w_lora_a[i], rw_w_lora_b[i], rw_a0[i],
                rw_a_lora_a[i], rw_a_lora_b[i], rw_g_lora_a[i], rw_g_lora_b[i], seq=S)
            y = _rwkv_scan(r, lw, k, v, a, rw_k_k[i], rw_k_a[i], rw_r_k[i], rw_gn_w[i], rw_gn_b[i],
                           batch=B, seq=S)
            mixer = (y, g, rw_w_o[i])
        else:
            i = l - n_rwkv
            q, qr, gates = _nsa_q(xt, norm_mix[l], nsa_w_q[i], nsa_q_norm[i], seq=S)
            bounds = jnp.stack([_logit_bound(nsa_q_norm[i], kv_k_norm[br]) for br in range(3)])
            o = _nsa_attn(q, qr, gates, bounds, *shared, batch=B, seq=S)
            mixer = (o, None, nsa_w_o[i])
        xt = _ffn(xt, norm_ffn2[l], ffn2_w_gate, ffn2_w_up, ffn2_w_down, l, mixer=mixer)
        if l == n_rwkv - 1:
            kc_tok, vc_tok, k_sel, v_sel, k_win, v_win = _nsa_kv(xt, kv_norm, kv_w, kv_k_norm, batch=B, seq=S)
            k_cmp, v_cmp = _nsa_cmp(kc_tok, vc_tok, kv_k_norm[0], cmp_pos_k, cmp_k_w1, cmp_k_w2,
                                    cmp_pos_v, cmp_v_w1, cmp_v_w2, batch=B, seq=S)
            shared = (k_cmp, v_cmp, k_sel, v_sel, k_win, v_win)
    return xt.reshape(B, S, D)
```

```python
import functools

import numpy as np
import jax
import jax.numpy as jnp
from jax import lax
from jax.experimental import pallas as pl
from jax.experimental.pallas import tpu as pltpu

F32 = jnp.float32
BF16 = jnp.bfloat16

NORM_EPS = 1e-6
RW_HEAD_DIM = 64
RW_GN_EPS = 64e-5
NSA_HEADS = 16
NSA_GROUPS = 4
HEADS_PER_GROUP = NSA_HEADS // NSA_GROUPS
HEAD_DIM = 64
L_CMP = 32
D_CMP = 16
L_SEL = 64
N_SEL = 16
WINDOW = 512
ROPE_THETA = 10000.0
FORCE_SCORE = 1e6
NEG_INF = -1e30
LOGIT_SCALE = HEAD_DIM ** -0.5 * float(np.log2(np.e))

LANES = 128
SUBLANES = 8
PACKED_SUBLANES = 16
VMEM_LIMIT = 56 * 1024 * 1024

CHUNK = 64
INV_BLOCK = 16
Q_TILE = 256
GATE_ROWS = PACKED_SUBLANES


def _cparams(sem):
    return pltpu.CompilerParams(dimension_semantics=sem, vmem_limit_bytes=VMEM_LIMIT)


def _dot(a, b):
    return jnp.dot(a, b, preferred_element_type=F32)


def _dot_nt(a, b):
    return lax.dot_general(a, b, (((1,), (1,)), ((), ())), preferred_element_type=F32)


def _dot_tn(a, b):
    return lax.dot_general(a, b, (((0,), (0,)), ((), ())), preferred_element_type=F32)


def _split3(x):
    hi = x.astype(BF16)
    r1 = x - hi.astype(F32)
    mid = r1.astype(BF16)
    lo = (r1 - mid.astype(F32)).astype(BF16)
    return hi, mid, lo


def _sel_left(m01, x):
    x1, x2, x3 = _split3(x)
    return _dot(m01, x1) + (_dot(m01, x2) + _dot(m01, x3))


def _rms(x, g):
    return x * lax.rsqrt(jnp.mean(x * x, axis=-1, keepdims=True) + NORM_EPS) * g


def _sigmoid(z):
    return 1.0 / (1.0 + jnp.exp(-z))


def _rope(x, cos, sin_signed):
    n = x.shape[-1]
    half = HEAD_DIM // 2
    lane = lax.broadcasted_iota(jnp.int32, x.shape, 1)
    first = (lane & (HEAD_DIM - 1)) < half
    partner = jnp.where(first, pltpu.roll(x, n - half, axis=1), pltpu.roll(x, half, axis=1))
    return x * cos + partner * sin_signed


def _ffn_kernel(*refs, mixer, gated):
    refs = list(refs)
    x = refs.pop(0)[...]
    if mixer:
        y = refs.pop(0)[...]
        if gated:
            y = y * refs.pop(0)[...]
        x = x + _dot(y.astype(BF16), refs.pop(0)[...])
    g_ref, wg_ref, wu_ref, wd_ref, o_ref = refs
    h = _rms(x, g_ref[...]).astype(BF16)
    gate = _dot(h, wg_ref[...])
    up = _dot(h, wu_ref[...])
    act = (gate * _sigmoid(gate) * up).astype(BF16)
    o_ref[...] = x + 0.5 * _dot(act, wd_ref[...])


def _resident(shape):
    return pl.BlockSpec(shape, lambda *_: (0,) * len(shape), pipeline_mode=pl.Buffered(1))


def _resident_layer(stacked, layer):
    return pl.BlockSpec((None,) + stacked.shape[1:], lambda *_: (layer,) + (0,) * (stacked.ndim - 1),
                        pipeline_mode=pl.Buffered(1))


def _ffn(x, g, wg, wu, wd, layer, *, mixer=None, tm=512):
    T, D = x.shape
    tm = min(tm, T)
    tile = pl.BlockSpec((tm, D), lambda i: (i, 0))
    ins, specs = [x], [tile]
    if mixer is not None:
        y, y_gate, w_o = mixer
        ins += [y] + ([y_gate] if y_gate is not None else []) + [w_o.astype(BF16)]
        specs += [tile] * (len(ins) - 2) + [_resident(w_o.shape)]
    ins += [g.reshape(1, D), wg.astype(BF16), wu.astype(BF16), wd.astype(BF16)]
    specs += [_resident((1, D)), _resident_layer(wg, layer), _resident_layer(wu, layer), _resident_layer(wd, layer)]
    return pl.pallas_call(
        functools.partial(_ffn_kernel, mixer=mixer is not None, gated=mixer is not None and mixer[1] is not None),
        grid=(T // tm,),
        in_specs=specs,
        out_specs=tile,
        out_shape=jax.ShapeDtypeStruct((T, D), F32),
        compiler_params=_cparams(("parallel",)),
        name="ffn",
    )(*ins)


def _rwkv_prep_kernel(x_ref, xp_ref, gm_ref, mu_ref, wrkv_ref, w0_ref, wla_ref, wlb_ref, a0_ref,
                      ala_ref, alb_ref, gla_ref, glb_ref,
                      r_ref, lw_ref, k_ref, v_ref, a_ref, g_ref, *, tiles_per_seq):
    i = pl.program_id(0)
    gm = gm_ref[...]
    h = _rms(x_ref[...], gm)
    hp = _rms(xp_ref[...], gm)[SUBLANES - 1:SUBLANES, :]
    hp = jnp.where(i % tiles_per_seq == 0, 0.0, hp)
    row = lax.broadcasted_iota(jnp.int32, h.shape, 0)
    prev = jnp.where(row == 0, hp, pltpu.roll(h, 1, axis=0))
    xx = prev - h
    mu = mu_ref[...]
    mix = lambda c: (h + xx * mu[c:c + 1, :]).astype(BF16)
    r_ref[...] = _dot(mix(0), wrkv_ref[0])
    k_ref[...] = _dot(mix(2), wrkv_ref[1])
    v_ref[...] = _dot(mix(3), wrkv_ref[2])
    z = w0_ref[...] + _dot(jnp.tanh(_dot(mix(1), wla_ref[...])).astype(BF16), wlb_ref[...])
    lw_ref[...] = -float(np.exp(-0.5) * np.log2(np.e)) * _sigmoid(z)
    a_ref[...] = _sigmoid(a0_ref[...] + _dot(_dot(mix(4), ala_ref[...]).astype(BF16), alb_ref[...]))
    g_ref[...] = _dot(_sigmoid(_dot(mix(5), gla_ref[...])).astype(BF16), glb_ref[...])


def _rwkv_prep(x, gmix, mu, w_rkv, w0, wla, wlb, a0, ala, alb, gla, glb, *, seq, tm=512):
    T, D = x.shape
    tm = min(tm, seq)
    full = lambda a: _resident(a.shape)
    row = lambda a: a.reshape(1, D)
    args = [gmix.reshape(1, D), mu, w_rkv.astype(BF16), row(w0), wla.astype(BF16), wlb.astype(BF16),
            row(a0), ala.astype(BF16), alb.astype(BF16), gla.astype(BF16), glb.astype(BF16)]
    out = jax.ShapeDtypeStruct((T, D), F32)
    return pl.pallas_call(
        functools.partial(_rwkv_prep_kernel, tiles_per_seq=seq // tm),
        grid=(T // tm,),
        in_specs=[pl.BlockSpec((tm, D), lambda i: (i, 0)),
                  pl.BlockSpec((SUBLANES, D), lambda i: (jnp.maximum(i * (tm // SUBLANES) - 1, 0), 0))]
                 + [full(a) for a in args],
        out_specs=[pl.BlockSpec((tm, D), lambda i: (i, 0))] * 6,
        out_shape=[out] * 6,
        compiler_params=_cparams(("parallel",)),
        name="rwkv_prep",
    )(x, x, *args)


PAIR = 2 * RW_HEAD_DIM


def _blockdiag(x, masks):
    return jnp.concatenate([x * masks[0], x * masks[1]], axis=0)


def _pair_dot(lhs, rhs_list, masks, transpose_rhs=False):
    dot = _dot_nt if transpose_rhs else _dot
    rhs = jnp.concatenate([_blockdiag(r.astype(BF16), masks) for r in rhs_list], axis=0 if transpose_rhs else 1)
    out = dot(lhs.astype(BF16), rhs)
    return [out[:, i * PAIR:(i + 1) * PAIR] for i in range(len(rhs_list))]


def _seg_sum(x, is_a):
    zero = jnp.zeros_like(x)
    sa = jnp.sum(jnp.where(is_a, x, zero), axis=-1, keepdims=True)
    sb = jnp.sum(jnp.where(is_a, zero, x), axis=-1, keepdims=True)
    return jnp.where(is_a, sa, sb)


def _rwkv_chunk(r, lw, k, v, a, kk_p, ka_p, rk_p, gn_w, gn_b, state, consts):
    is_a, masks, tri_incl, tri_strict, blockdiag, eye, ltri = consts
    inv_n = 1.0 / RW_HEAD_DIM
    kkr = k * kk_p
    kk = kkr * jnp.minimum(lax.rsqrt(_seg_sum(kkr * kkr, is_a)), 1e12)
    k2 = k * (1.0 + (a - 1.0) * ka_p)
    kb = kk * a

    cum = _sel_left(ltri, lw)
    yield
    cum_last = cum[CHUNK - 1:CHUNK, :]
    einv = jnp.exp2(-cum)
    elast = jnp.exp2(cum_last - cum)
    a_t = -kk * jnp.exp2(cum - lw)
    r_t = r * jnp.exp2(cum)
    b_h = kb * einv
    k_h = k2 * einv
    b_l = kb * elast
    k_l = k2 * elast

    ar = jnp.concatenate([a_t, r_t], axis=0)
    d1 = lambda p, qs: _pair_dot(p, qs, masks)
    sc_b, sc_k = _pair_dot(ar, [b_h, k_h], masks, transpose_rhs=True)
    yield
    m_ab = jnp.where(tri_strict, sc_b[:CHUNK], 0.0)
    m_rb = jnp.where(tri_incl, sc_b[CHUNK:], 0.0)
    m_ak = jnp.where(tri_strict, sc_k[:CHUNK], 0.0)
    m_rk = jnp.where(tri_incl, sc_k[CHUNK:], 0.0)

    (w1,) = d1(m_ak, [v])
    m_d = jnp.where(blockdiag, m_ab, 0.0)
    m_o = m_ab - m_d
    dinv = eye + m_d
    (pw,) = d1(m_d, [m_d])
    yield
    for _ in range(int(np.log2(INV_BLOCK)) - 2):
        (both,) = d1(jnp.concatenate([pw, dinv], axis=0), [pw])
        pw, dinv = both[:CHUNK], dinv + both[CHUNK:]
        yield
    dinv = dinv + d1(dinv, [pw])[0]
    yield
    n1, x1, x2 = d1(dinv, [m_o, a_t, w1])
    yield
    n2, u1, u2 = d1(n1, [n1, x1, x2])
    x1, x2 = x1 + u1, x2 + u2
    yield
    u1, u2 = d1(n2, [x1, x2])
    x1, x2 = x1 + u1, x2 + u2
    yield

    x1b, x2b, vb = x1.astype(BF16), x2.astype(BF16), v.astype(BF16)
    bd_v = _blockdiag(vb, masks)
    zy = _dot(jnp.concatenate([m_rb, m_rk], axis=1).astype(BF16),
              jnp.concatenate([jnp.concatenate([_blockdiag(x1b, masks), _blockdiag(x2b, masks)], axis=1),
                               jnp.concatenate([jnp.zeros_like(bd_v), bd_v], axis=1)], axis=0))
    g1 = r_t + zy[:, :PAIR]
    y_c = zy[:, PAIR:]
    z2 = _dot_tn(jnp.concatenate([b_l, k_l], axis=0).astype(BF16),
                 jnp.concatenate([jnp.concatenate([x1b, x2b], axis=1),
                                  jnp.concatenate([jnp.zeros_like(vb), vb], axis=1)], axis=0))
    yield
    diag_blocks = lambda f: jnp.where(is_a, f[:RW_HEAD_DIM], f[RW_HEAD_DIM:])
    trans = jnp.where(eye > 0, jnp.exp2(cum_last), 0.0) + diag_blocks(z2[:, :PAIR])
    add = diag_blocks(z2[:, PAIR:])
    (ys,) = d1(jnp.concatenate([g1, trans], axis=0), [state])
    yield
    y = ys[:CHUNK] + y_c
    new_state = ys[CHUNK:] + add

    mean = _seg_sum(y, is_a) * inv_n
    dev = y - mean
    var = _seg_sum(dev * dev, is_a) * inv_n
    yn = dev * lax.rsqrt(var + RW_GN_EPS) * gn_w + gn_b
    bonus = _seg_sum(r * k2 * rk_p, is_a) * v
    return yn + bonus, new_state


def _rwkv_scan_kernel(r_ref, lw_ref, k_ref, v_ref, a_ref, kk_ref, ka_ref, rk_ref, gw_ref, gb_ref, o_ref,
                      state_ref, *, n_chunks, pairs):
    @pl.when(pl.program_id(2) == 0)
    def _():
        state_ref[...] = jnp.zeros_like(state_ref)

    row = lax.broadcasted_iota(jnp.int32, (CHUNK, PAIR), 0)
    lane = lax.broadcasted_iota(jnp.int32, (CHUNK, PAIR), 1)
    col = lane & (RW_HEAD_DIM - 1)
    is_a = lane < RW_HEAD_DIM
    blockdiag = (row // INV_BLOCK) == (col // INV_BLOCK)
    eye = jnp.where(row == col, 1.0, 0.0).astype(F32)
    row_s = lax.broadcasted_iota(jnp.int32, (CHUNK, CHUNK), 0)
    col_s = lax.broadcasted_iota(jnp.int32, (CHUNK, CHUNK), 1)
    ltri = jnp.where(row_s >= col_s, 1.0, 0.0).astype(BF16)
    masks = (jnp.where(is_a, 1.0, 0.0).astype(BF16), jnp.where(is_a, 0.0, 1.0).astype(BF16))
    consts = (is_a, masks, row >= col, row > col, blockdiag, eye, ltri)

    chains = [(s, p) for s in range(r_ref.shape[0]) for p in range(pairs)]

    def body(c, carry):
        t0 = pl.multiple_of(c * CHUNK, CHUNK)
        gens = []
        for s, p in chains:
            sl = slice(p * PAIR, (p + 1) * PAIR)
            ld = lambda ref: ref[s, pl.ds(t0, CHUNK), sl]
            gens.append(_rwkv_chunk(ld(r_ref), ld(lw_ref), ld(k_ref), ld(v_ref), ld(a_ref),
                                    kk_ref[:, sl], ka_ref[:, sl], rk_ref[:, sl], gw_ref[:, sl], gb_ref[:, sl],
                                    state_ref[s * pairs + p], consts))
        results = [None] * len(chains)
        while any(res is None for res in results):
            for i, gen in enumerate(gens):
                if results[i] is None:
                    try:
                        next(gen)
                    except StopIteration as done:
                        results[i] = done.value
        for (s, p), (y, st) in zip(chains, results):
            state_ref[s * pairs + p] = st
            o_ref[s, pl.ds(t0, CHUNK), p * PAIR:(p + 1) * PAIR] = y
        return carry

    lax.fori_loop(0, n_chunks, body, 0)


def _rwkv_scan(r, lw, k, v, a, k_k, k_a, r_k, gn_w, gn_b, *, batch, seq, pairs=8, seqs=2, tt=256):
    T, D = r.shape
    tt = min(tt, seq)
    seqs = min(seqs, batch)
    lanes = pairs * PAIR
    seq_spec = pl.BlockSpec((seqs, tt, lanes), lambda b, p, t: (b, t, p))
    par_spec = pl.BlockSpec((1, lanes), lambda b, p, t: (0, p))
    as3 = lambda t: t.reshape(batch, seq, D)
    row = lambda t: t.reshape(1, D)
    y = pl.pallas_call(
        functools.partial(_rwkv_scan_kernel, n_chunks=tt // CHUNK, pairs=pairs),
        grid=(batch // seqs, D // lanes, seq // tt),
        in_specs=[seq_spec] * 5 + [par_spec] * 5,
        out_specs=seq_spec,
        out_shape=jax.ShapeDtypeStruct((batch, seq, D), F32),
        scratch_shapes=[pltpu.VMEM((seqs * pairs, RW_HEAD_DIM, PAIR), F32)],
        compiler_params=_cparams(("parallel", "parallel", "arbitrary")),
        name="rwkv_scan",
    )(as3(r), as3(lw), as3(k), as3(v), as3(a), row(k_k), row(k_a), row(r_k), row(gn_w), row(gn_b))
    return y.reshape(T, D)


def _group_rms(x, gain, pool, expand):
    pooled = _dot((x * x).astype(BF16), pool)
    inv = lax.rsqrt(pooled * (1.0 / HEAD_DIM) + NORM_EPS)
    hi = inv.astype(BF16)
    lo = (inv - hi.astype(F32)).astype(BF16)
    return x * (_dot(hi, expand) + _dot(lo, expand)) * gain


def _nsa_kv_kernel(x_ref, gn_ref, w_ref, kn_ref, cos_ref, sin_ref, pool_ref, exp_ref,
                   kc_ref, vc_ref, ks_ref, vs_ref, kw_ref, vw_ref):
    h = _rms(x_ref[...], gn_ref[...]).astype(BF16)
    kv = _dot(h, w_ref[...])
    W = NSA_GROUPS * HEAD_DIM
    part = lambda i: kv[:, i * W:(i + 1) * W]
    cos, sin = cos_ref[...], sin_ref[...]
    pool, expand = pool_ref[...], exp_ref[...]

    def put(ref, val):
        for g in range(NSA_GROUPS):
            ref[g] = val[:, g * HEAD_DIM:(g + 1) * HEAD_DIM].astype(ref.dtype)

    put(kc_ref, part(0))
    put(vc_ref, part(1))
    put(ks_ref, _rope(_group_rms(part(2), kn_ref[1:2, :], pool, expand), cos, sin))
    put(vs_ref, part(3))
    put(kw_ref, _rope(_group_rms(part(4), kn_ref[2:3, :], pool, expand), cos, sin))
    put(vw_ref, part(5))


def _pool_matrices(width):
    head = np.arange(width) // HEAD_DIM
    pool = (head[:, None] == np.arange(LANES)[None, :]).astype(np.float32)
    return jnp.asarray(pool, BF16), jnp.asarray(pool.T, BF16)


def _rope_tables(seq, width):
    half = HEAD_DIM // 2
    inv = ROPE_THETA ** (-jnp.arange(half, dtype=F32) / half)
    ang = jnp.arange(seq).astype(F32)[:, None] * inv[None, :]
    cos, sin = jnp.cos(ang), jnp.sin(ang)
    reps = width // HEAD_DIM
    return (jnp.tile(jnp.concatenate([cos, cos], axis=1), (1, reps)),
            jnp.tile(jnp.concatenate([-sin, sin], axis=1), (1, reps)))


def _nsa_kv(x, kv_norm, kv_w, kv_k_norm, *, batch, seq, tm=512):
    T, D = x.shape
    tm = min(tm, seq)
    G, dh = NSA_GROUPS, HEAD_DIM
    W = G * dh
    cos, sin = _rope_tables(seq, W)
    pool, expand = _pool_matrices(W)
    kn = jnp.tile(kv_k_norm, (1, G))
    tps = seq // tm
    full = lambda a: pl.BlockSpec(a.shape, lambda b, t: (0,) * a.ndim)
    tab = pl.BlockSpec((tm, W), lambda b, t: (t, 0))
    out_tile = pl.BlockSpec((None, G, tm, dh), lambda b, t: (b, 0, t, 0))
    w = kv_w.astype(BF16)
    shape = (batch, G, seq, dh)
    return pl.pallas_call(
        _nsa_kv_kernel,
        grid=(batch, tps),
        in_specs=[pl.BlockSpec((tm, D), lambda b, t: (b * tps + t, 0)), pl.BlockSpec((1, D), lambda b, t: (0, 0)),
                  full(w), full(kn), tab, tab, full(pool), full(expand)],
        out_specs=[out_tile] * 6,
        out_shape=[jax.ShapeDtypeStruct(shape, F32)] * 2 + [jax.ShapeDtypeStruct(shape, BF16)] * 4,
        compiler_params=_cparams(("parallel", "parallel")),
        name="nsa_kv",
    )(x, kv_norm.reshape(1, D), w, kn, cos, sin, pool, expand)


def _compress(tok_ref, pos_ref, w1_ref, w2_ref):
    nch = tok_ref.shape[0] // D_CMP
    dh = tok_ref.shape[1]
    lo = hi = None
    for l in range(D_CMP):
        rows = tok_ref[pl.ds(l, nch, stride=D_CMP), :]
        for half in range(L_CMP // D_CMP):
            p = half * D_CMP + l
            term = _dot((rows + pos_ref[p:p + 1, :]).astype(BF16), w1_ref[p * dh:(p + 1) * dh, :])
            if half == 0:
                lo = term if lo is None else lo + term
            else:
                hi = term if hi is None else hi + term
    hid = lo + pltpu.roll(hi, nch - 1, axis=0)
    act = 0.5 * hid * (1.0 + jnp.tanh(np.sqrt(2.0 / np.pi) * (hid + 0.044715 * hid * hid * hid)))
    return _dot(act.astype(BF16), w2_ref[...])


def _nsa_cmp_kernel(ck_ref, cv_ref, pk_ref, pv_ref, w1k_ref, w2k_ref, w1v_ref, w2v_ref, kn_ref,
                    ko_ref, vo_ref):
    ko_ref[...] = _rms(_compress(ck_ref, pk_ref, w1k_ref, w2k_ref), kn_ref[...])
    vo_ref[...] = _compress(cv_ref, pv_ref, w1v_ref, w2v_ref)


def _nsa_cmp(kc_tok, vc_tok, kn0, pos_k, w1k, w2k, pos_v, w1v, w2v, *, batch, seq):
    G, dh = NSA_GROUPS, HEAD_DIM
    assert L_CMP == 2 * D_CMP
    nch = seq // D_CMP
    full = lambda a: pl.BlockSpec(a.shape, lambda b, g: (0,) * a.ndim)
    blk = pl.BlockSpec((None, None, seq, dh), lambda b, g: (b, g, 0, 0))
    oblk = pl.BlockSpec((None, nch, dh), lambda b, g: (b * G + g, 0, 0))
    args = [pos_k, pos_v, w1k.astype(BF16), w2k.astype(BF16), w1v.astype(BF16), w2v.astype(BF16),
            kn0.reshape(1, dh)]
    return pl.pallas_call(
        _nsa_cmp_kernel,
        grid=(batch, G),
        in_specs=[blk, blk] + [full(a) for a in args],
        out_specs=[oblk, oblk],
        out_shape=[jax.ShapeDtypeStruct((batch * G, nch, dh), F32)] * 2,
        compiler_params=_cparams(("parallel", "parallel")),
        name="nsa_cmp",
    )(kc_tok, vc_tok, *args)


def _nsa_q_kernel(x_ref, gn_ref, wq_ref, wg_ref, qn_ref, cos_ref, sin_ref, pool_ref, exp_ref,
                  q_ref, qr_ref, gate_ref):
    h = _rms(x_ref[...], gn_ref[...]).astype(BF16)
    q = _group_rms(_dot(h, wq_ref[...]), qn_ref[...], pool_ref[...], exp_ref[...])
    reps = q.shape[1] // cos_ref.shape[1]
    cos = jnp.concatenate([cos_ref[...]] * reps, axis=1)
    sin = jnp.concatenate([sin_ref[...]] * reps, axis=1)
    scale = LOGIT_SCALE
    q_ref[...] = (q * scale).astype(BF16)
    qr_ref[...] = (_rope(q, cos, sin) * scale).astype(BF16)
    gate_ref[...] = _sigmoid(_dot(h, wg_ref[...]))


def _nsa_q(x, gmix, w_q, q_norm, *, seq, tm=512):
    T, D = x.shape
    tm = min(tm, seq)
    HD = NSA_HEADS * HEAD_DIM
    cos, sin = _rope_tables(seq, LANES)
    pool, expand = _pool_matrices(HD)
    wq = w_q[:, :HD].astype(BF16)
    wg = jnp.pad(w_q[:, HD:], ((0, 0), (0, LANES - 3 * NSA_HEADS))).astype(BF16)
    qn = jnp.tile(q_norm.reshape(1, HEAD_DIM), (1, NSA_HEADS))
    tps = seq // tm
    full = lambda a: pl.BlockSpec(a.shape, lambda i: (0,) * a.ndim)
    tab = pl.BlockSpec((tm, LANES), lambda i: (i % tps, 0))
    tile = lambda w: pl.BlockSpec((tm, w), lambda i: (i, 0))
    return pl.pallas_call(
        _nsa_q_kernel,
        grid=(T // tm,),
        in_specs=[tile(D), pl.BlockSpec((1, D), lambda i: (0, 0)), full(wq), full(wg), full(qn),
                  tab, tab, full(pool), full(expand)],
        out_specs=[tile(HD), tile(HD), tile(LANES)],
        out_shape=[jax.ShapeDtypeStruct((T, HD), BF16)] * 2 + [jax.ShapeDtypeStruct((T, LANES), F32)],
        compiler_params=_cparams(("parallel",)),
        name="nsa_q",
    )(x, gmix.reshape(1, D), wq, wg, qn, cos, sin, pool, expand)


RANK_ROWS = SUBLANES
MAX_SAFE_SHIFT = 43.0


def _topk_bias(score, n_top):
    nsel = score.shape[0]
    groups = [score[lo:lo + RANK_ROWS, :] for lo in range(0, nsel, RANK_ROWS)]
    ranks = [jnp.zeros(g.shape, jnp.int32) for g in groups]
    for i in range(nsel):
        si = score[i:i + 1, :]
        for gi, sg in enumerate(groups):
            lo = gi * RANK_ROWS
            if lo > i:
                beats = si >= sg
            elif lo + RANK_ROWS - 1 < i:
                beats = si > sg
            else:
                later = lax.broadcasted_iota(jnp.int32, sg.shape, 0) + lo > i
                beats = (si > sg) | ((si == sg) & later)
            ranks[gi] = ranks[gi] + jnp.where(beats, 1, 0)
    return [jnp.where(r < n_top, 0.0, NEG_INF) for r in ranks]


def _nsa_attn_kernel(q_ref, qr_ref, gate_ref, kc_ref, vc_ref, ks_ref, vs_ref, kw_ref, vw_ref, msel_ref,
                     wbias_ref, bound_ref, o_ref, bias_ref):
    step = functools.partial(_nsa_attn_step, q_ref, qr_ref, gate_ref, kc_ref, vc_ref, ks_ref, vs_ref, kw_ref,
                             vw_ref, msel_ref, wbias_ref, bound_ref, o_ref, bias_ref)
    safe = ((bound_ref[0] <= MAX_SAFE_SHIFT) & (bound_ref[1] <= MAX_SAFE_SHIFT)
            & (bound_ref[2] <= MAX_SAFE_SHIFT))
    pl.when(safe)(functools.partial(step, use_bound=True))
    pl.when(jnp.logical_not(safe))(functools.partial(step, use_bound=False))


def _nsa_attn_step(q_ref, qr_ref, gate_ref, kc_ref, vc_ref, ks_ref, vs_ref, kw_ref, vw_ref, msel_ref,
                   wbias_ref, bound_ref, o_ref, bias_ref, *, use_bound):
    TQ, HPG, dh = Q_TILE, HEADS_PER_GROUP, HEAD_DIM
    NQ = HPG * TQ
    qi = pl.program_id(2)
    s0 = qi * TQ
    heads = lambda ref: [ref[:, hh * dh:(hh + 1) * dh] for hh in range(HPG)]
    q_heads, qr_heads = heads(q_ref), heads(qr_ref)
    qk = lambda keys, qs: jnp.concatenate([_dot_nt(keys, qh) for qh in qs], axis=1)
    over_heads = lambda a: jnp.concatenate([a] * HPG, axis=1)
    rows = lambda shape: lax.broadcasted_iota(jnp.int32, shape, 0)
    cols = lambda shape: lax.broadcasted_iota(jnp.int32, shape, 1)

    def shifted(masked_logits, bound):
        if use_bound:
            return masked_logits(bound)
        s = masked_logits(0.0)
        return s - jnp.max(s, axis=0, keepdims=True)

    def softmax_then(s, finish):
        e = jnp.exp2(s)
        return finish(e, jnp.sum(e, axis=0, keepdims=True))

    ncp = kc_ref.shape[0]
    shape = (ncp, TQ)
    cmp_bias = jnp.where(rows(shape) * D_CMP + (L_CMP - 1) <= s0 + cols(shape), 0.0, NEG_INF)
    sees_any = s0 + (cols((1, NQ)) & (TQ - 1)) >= L_CMP - 1

    def cmp_finish(e, l):
        p = e * jnp.where(sees_any, 1.0 / l, 0.0)
        imp = p[:, 0:TQ]
        for hh in range(1, HPG):
            imp = imp + p[:, hh * TQ:(hh + 1) * TQ]
        return _dot_tn(vc_ref[...].astype(BF16), p.astype(BF16)), imp

    s_cmp = shifted(lambda shift: qk(kc_ref[...].astype(BF16), q_heads) + over_heads(cmp_bias - shift),
                    bound_ref[0])
    o_cmp, imp = softmax_then(s_cmp, cmp_finish)

    span = WINDOW + TQ
    start = pl.multiple_of(jnp.maximum(s0 - WINDOW, 0), TQ)
    s_win = shifted(lambda shift: qk(kw_ref[pl.ds(start, span), :], qr_heads) + over_heads(wbias_ref[...] - shift),
                    bound_ref[2])
    o_win = softmax_then(s_win, lambda e, l: _dot_tn(vw_ref[pl.ds(start, span), :], e.astype(BF16)) / l)

    p_slc = _sel_left(msel_ref[...], imp)
    nsel = p_slc.shape[0]
    shape = (nsel, TQ)
    blk = rows(shape)
    t = s0 + cols(shape)
    cur = t >> int(np.log2(L_SEL))
    forced = (blk == 0) | (blk == cur) | (blk == cur - 1)
    score = jnp.where(forced, FORCE_SCORE, p_slc)
    score = jnp.where(blk * L_SEL > t, -1.0, score)
    for gi, group_bias in enumerate(_topk_bias(score, min(N_SEL, nsel))):
        bias_ref[gi] = group_bias

    KT = RANK_ROWS * L_SEL
    last = (qi * TQ) // KT
    shape = (KT, TQ)
    causal_bias = jnp.where(rows(shape) <= cols(shape) + (s0 - last * KT), 0.0, NEG_INF)

    def tile_logits(j):
        return qk(ks_ref[pl.ds(pl.multiple_of(j * KT, KT), KT), :], qr_heads)

    def sel_slabs(logits, j, diagonal, shift):
        bias = bias_ref[j]
        for b in range(RANK_ROWS):
            blk_rows = slice(b * L_SEL, (b + 1) * L_SEL)
            mask = bias[b:b + 1, :] + causal_bias[blk_rows, :] if diagonal else bias[b:b + 1, :]
            yield logits[blk_rows, :] + (over_heads(mask) - shift)

    def sel_max(j, m, diagonal):
        for s in sel_slabs(tile_logits(j), j, diagonal, 0.0):
            m = jnp.maximum(m, jnp.max(s, axis=0, keepdims=True))
        return m

    def true_max():
        m = lax.fori_loop(0, last, lambda j, m: sel_max(j, m, False), jnp.full((1, NQ), NEG_INF, F32))
        return sel_max(last, m, True)

    m_sel = jnp.full((1, NQ), bound_ref[1], F32) if use_bound else true_max()

    def sel_accumulate(tiles, carry, diagonal):
        l, acc = carry
        for j, logits in [(j, tile_logits(j)) for j in tiles]:
            ps = []
            for s in sel_slabs(logits, j, diagonal, m_sel):
                p = jnp.exp2(s)
                l = l + jnp.sum(p.reshape(L_SEL // SUBLANES, SUBLANES, NQ), axis=0)
                ps.append(p.astype(BF16))
            acc = acc + _dot_tn(vs_ref[pl.ds(pl.multiple_of(j * KT, KT), KT), :], jnp.concatenate(ps, axis=0))
        return l, acc

    carry = (jnp.zeros((SUBLANES, NQ), F32), jnp.zeros((dh, NQ), F32))
    carry = lax.fori_loop(0, last // 2, lambda i, c: sel_accumulate([2 * i, 2 * i + 1], c, False), carry)
    carry = lax.fori_loop(2 * (last // 2), last, lambda j, c: sel_accumulate([j], c, False), carry)
    l_sel, acc_sel = sel_accumulate([last], carry, True)
    o_sel = acc_sel / jnp.sum(l_sel, axis=0, keepdims=True)

    outs = []
    for hh in range(HPG):
        sl = slice(hh * TQ, (hh + 1) * TQ)
        g = lambda c: gate_ref[3 * hh + c:3 * hh + c + 1, :]
        o_h = g(0) * o_cmp[:, sl] + g(1) * o_sel[:, sl] + g(2) * o_win[:, sl]
        outs.append(o_h.T)
    o_ref[...] = jnp.concatenate(outs, axis=1)


def _selection_matrix(seq):
    nsel, ncp = seq // L_SEL, seq // D_CMP
    stride = L_SEL // D_CMP
    lpad = L_CMP // D_CMP - 1
    n = np.arange(ncp)[None, :]
    j = np.arange(nsel)[:, None]
    m = (n >= stride * j - lpad) & (n <= stride * j + stride - 1)
    return jnp.asarray(m.astype(np.float32), BF16)


def _window_bias():
    r = np.arange(WINDOW + Q_TILE)[:, None]
    c = np.arange(Q_TILE)[None, :]
    offs = [min(i * Q_TILE, WINDOW) for i in range(WINDOW // Q_TILE + 1)]
    masks = [np.where((r <= c + off) & (r > c + off - WINDOW), 0.0, NEG_INF) for off in offs]
    return jnp.asarray(np.stack(masks), F32)


def _nsa_attn(q, qr, gates, bounds, k_cmp, v_cmp, k_sel, v_sel, k_win, v_win, *, batch, seq):
    G, HPG, dh = NSA_GROUPS, HEADS_PER_GROUP, HEAD_DIM
    T = batch * seq
    nt = seq // Q_TILE
    gw = G * HPG * 3
    gt = jnp.transpose(gates[:, :gw].reshape(batch, seq, G, HPG * 3), (0, 2, 3, 1))
    gt = jnp.pad(gt, ((0, 0), (0, 0), (0, GATE_ROWS - HPG * 3), (0, 0)))
    ncp = seq // D_CMP
    msel = _selection_matrix(seq)
    wbias = _window_bias()
    key_tile = RANK_ROWS * L_SEL
    assert seq % key_tile == 0 and key_tile % Q_TILE == 0
    qspec = pl.BlockSpec((Q_TILE, HPG * dh), lambda b, g, i: (b * nt + i, g))
    cspec = pl.BlockSpec((None, ncp, dh), lambda b, g, i: (b * G + g, 0, 0))
    kspec = pl.BlockSpec((None, None, seq, dh), lambda b, g, i: (b, g, 0, 0))
    return pl.pallas_call(
        _nsa_attn_kernel,
        grid=(batch, G, nt),
        in_specs=[qspec, qspec, pl.BlockSpec((None, None, GATE_ROWS, Q_TILE), lambda b, g, i: (b, g, 0, i)),
                  cspec, cspec, kspec, kspec, kspec, kspec,
                  pl.BlockSpec(msel.shape, lambda b, g, i: (0, 0)),
                  pl.BlockSpec((None,) + wbias.shape[1:],
                               lambda b, g, i: (jnp.minimum(i, wbias.shape[0] - 1), 0, 0)),
                  pl.BlockSpec(memory_space=pltpu.SMEM)],
        out_specs=qspec,
        out_shape=jax.ShapeDtypeStruct((T, G * HPG * dh), F32),
        scratch_shapes=[pltpu.VMEM((seq // key_tile, RANK_ROWS, Q_TILE), F32)],
        compiler_params=_cparams(("parallel", "parallel", "arbitrary")),
        name="nsa_attn",
    )(q, qr, gt, k_cmp, v_cmp, k_sel, v_sel, k_win, v_win, msel, wbias, bounds)


def _logit_bound(q_gain, k_gain):
    rounding = 1.02
    return (HEAD_DIM * LOGIT_SCALE * rounding) * jnp.max(jnp.abs(q_gain)) * jnp.max(jnp.abs(k_gain))


def kernel(x, norm_ffn1, ffn1_w_gate, ffn1_w_up, ffn1_w_down, norm_mix, norm_ffn2, ffn2_w_gate, ffn2_w_up, ffn2_w_down, rw_mu, rw_w_rkv, rw_w0, rw_w_lora_a, rw_w_lora_b, rw_a0, rw_a_lora_a, rw_a_lora_b, rw_g_lora_a, rw_g_lora_b, rw_k_k, rw_k_a, rw_r_k, rw_gn_w, rw_gn_b, rw_w_o, kv_norm, kv_w, kv_k_norm, cmp_pos_k, cmp_k_w1, cmp_k_w2, cmp_pos_v, cmp_v_w1, cmp_v_w2, nsa_w_q, nsa_q_norm, nsa_w_o):
    B, S, D = x.shape
    depth = norm_ffn1.shape[0]
    n_rwkv = rw_mu.shape[0]
    assert S % max(Q_TILE, CHUNK) == 0 and S >= WINDOW + Q_TILE and D % LANES == 0
    xt = x.reshape(B * S, D)
    shared = None
    for l in range(depth):
        xt = _ffn(xt, norm_ffn1[l], ffn1_w_gate, ffn1_w_up, ffn1_w_down, l)
        if l < n_rwkv:
            i = l
            r, lw, k, v, a, g = _rwkv_prep(
                xt, norm_mix[l], rw_mu[i], rw_w_rkv[i], rw_w0[i], rw_w_lora_a[i], rw_w_lora_b[i], rw_a0[i],
                rw_a_lora_a[i], rw_a_lora_b[i], rw_g_lora_a[i], rw_g_lora_b[i], seq=S)
            y = _rwkv_scan(r, lw, k, v, a, rw_k_k[i], rw_k_a[i], rw_r_k[i], rw_gn_w[i], rw_gn_b[i],
                           batch=B, seq=S)
            mixer = (y, g, rw_w_o[i])
        else:
            i = l - n_rwkv
            q, qr, gates = _nsa_q(xt, norm_mix[l], nsa_w_q[i], nsa_q_norm[i], seq=S)
            bounds = jnp.stack([_logit_bound(nsa_q_norm[i], kv_k_norm[br]) for br in range(3)])
            o = _nsa_attn(q, qr, gates, bounds, *shared, batch=B, seq=S)
            mixer = (o, None, nsa_w_o[i])
        xt = _ffn(xt, norm_ffn2[l], ffn2_w_gate, ffn2_w_up, ffn2_w_down, l, mixer=mixer)
        if l == n_rwkv - 1:
            kc_tok, vc_tok, k_sel, v_sel, k_win, v_win = _nsa_kv(xt, kv_norm, kv_w, kv_k_norm, batch=B, seq=S)
            k_cmp, v_cmp = _nsa_cmp(kc_tok, vc_tok, kv_k_norm[0], cmp_pos_k, cmp_k_w1, cmp_k_w2,
                                    cmp_pos_v, cmp_v_w1, cmp_v_w2, batch=B, seq=S)
            shared = (k_cmp, v_cmp, k_sel, v_sel, k_win, v_win)
    return xt.reshape(B, S, D)
```
